```python
import jax, jax.numpy as jnp
from jax import lax
import numpy as np

D_MODEL = 1024
BATCH = 16
SEQ = 2048
DEPTH = 2

CONV_GROUPS = 8
CONV_GROUP_DIM = 64
CONV_DIM = CONV_GROUPS * CONV_GROUP_DIM
CONV_WIDTH = 3
GLA_HEADS = 4
GLA_DK = 64
GLA_DV = 128
GLA_RANK = 16
GLA_TAU = 16.0
GLA_CHUNK = 64
SWA_HEADS = 8
SWA_KV_HEADS = 2
SWA_HEAD_DIM = 64
SWA_WINDOW = 128
SWA_BLOCK = 128
ROPE_THETA = 500000.0
ROPE_DIM = SWA_HEAD_DIM // 4
N_BRANCHES = 3
BRANCH_WIDTH = 512
D_FF = 4 * D_MODEL
NORM_EPS = 1e-6
SPLITS = (CONV_DIM, CONV_DIM, CONV_DIM,
          GLA_HEADS * GLA_DK, GLA_HEADS * GLA_DK, GLA_HEADS * GLA_DV, GLA_HEADS * GLA_DV, GLA_RANK,
          SWA_HEADS * SWA_HEAD_DIM, SWA_KV_HEADS * SWA_HEAD_DIM, SWA_KV_HEADS * SWA_HEAD_DIM,
          N_BRANCHES * D_MODEL)
N_IN = sum(SPLITS)

kernel_name = "hybrid_conv_gla_swa_parallel_gated"


def rms_norm(x, w):
    x32 = x.astype(jnp.float32)
    y = x32 * lax.rsqrt(jnp.mean(x32 * x32, axis=-1, keepdims=True) + NORM_EPS)
    return (y * w.astype(jnp.float32)).astype(x.dtype)


def short_conv_mixer(cx, cb, cc, conv_w, conv_b):
    u = cc * cx
    y = lax.conv_general_dilated(
        u, conv_w[:, None, :], window_strides=(1,), padding=[(CONV_WIDTH - 1, 0)],
        dimension_numbers=('NWC', 'WIO', 'NWC'), feature_group_count=CONV_DIM)
    return cb * (y + conv_b)


def gla_mixer(q, k, v, r, a, wa2, ba, onorm_w):
    f32 = jnp.float32
    Bsz, S, _ = q.shape
    N, L = S // GLA_CHUNK, GLA_CHUNK

    def heads(t, d):
        return t.reshape(Bsz, N, L, GLA_HEADS, d).transpose(0, 3, 1, 2, 4)

    q = heads(q.astype(f32) * GLA_DK ** -0.5, GLA_DK)
    k = heads(k.astype(f32), GLA_DK)
    v = heads(v.astype(f32), GLA_DV)
    glog = jax.nn.log_sigmoid((a @ wa2 + ba).astype(f32)) / GLA_TAU
    b = jnp.cumsum(heads(glog, GLA_DK), axis=3)
    b_last = b[:, :, :, -1:, :]
    q_in = q * jnp.exp(b)
    k_in = k * jnp.exp(-b)
    k_end = k * jnp.exp(b_last - b)
    causal = jnp.tril(jnp.ones((L, L), dtype=bool))
    att = jnp.where(causal, jnp.einsum('bhnid,bhnjd->bhnij', q_in, k_in), 0.0)
    o_intra = jnp.einsum('bhnij,bhnjv->bhniv', att, v)
    chunk_kv = jnp.einsum('bhnjd,bhnjv->bhndv', k_end, v)
    chunk_decay = jnp.exp(b_last[:, :, :, 0, :])

    def step(state, inp):
        kv_n, dec_n = inp
        return dec_n[..., None] * state + kv_n, state

    init = jnp.zeros((Bsz, GLA_HEADS, GLA_DK, GLA_DV), f32)
    _, states = lax.scan(step, init, (jnp.moveaxis(chunk_kv, 2, 0), jnp.moveaxis(chunk_decay, 2, 0)))
    states = jnp.moveaxis(states, 0, 2)
    o = o_intra + jnp.einsum('bhnid,bhndv->bhniv', q_in, states)
    o = o.transpose(0, 2, 3, 1, 4).reshape(Bsz, S, GLA_HEADS, GLA_DV)
    o = rms_norm(o, onorm_w).reshape(Bsz, S, GLA_HEADS * GLA_DV)
    return (o * jax.nn.silu(r.astype(f32))).astype(r.dtype)


def rope_partial(x, cos, sin):
    half = ROPE_DIM // 2
    x1, x2, rest = x[..., :half], x[..., half:ROPE_DIM], x[..., ROPE_DIM:]
    return jnp.concatenate([x1 * cos - x2 * sin, x2 * cos + x1 * sin, rest], axis=-1)


def with_prev_block(t):
    pad = ((0, 0), (1, 0)) + ((0, 0),) * (t.ndim - 2)
    prev = jnp.pad(t, pad)[:, :-1]
    return jnp.concatenate([prev, t], axis=2)


def swa_mixer(q, k, v, positions, qn_w, kn_w, sinks):
    f32 = jnp.float32
    Bsz, S, _ = q.shape
    G = SWA_HEADS // SWA_KV_HEADS
    N, T = S // SWA_BLOCK, SWA_BLOCK
    q = rms_norm(q.reshape(Bsz, S, SWA_HEADS, SWA_HEAD_DIM).astype(f32), qn_w)
    k = rms_norm(k.reshape(Bsz, S, SWA_KV_HEADS, SWA_HEAD_DIM).astype(f32), kn_w)
    v = v.reshape(Bsz, S, SWA_KV_HEADS, SWA_HEAD_DIM).astype(f32)
    inv_freq = ROPE_THETA ** (-jnp.arange(0, ROPE_DIM, 2, dtype=f32) / ROPE_DIM)
    ang = positions.astype(f32)[..., None] * inv_freq
    cos, sin = jnp.cos(ang)[:, :, None, :], jnp.sin(ang)[:, :, None, :]
    q = rope_partial(q, cos, sin) * SWA_HEAD_DIM ** -0.5
    k = rope_partial(k, cos, sin)
    qb = q.reshape(Bsz, N, T, SWA_KV_HEADS, G, SWA_HEAD_DIM)
    kw = with_prev_block(k.reshape(Bsz, N, T, SWA_KV_HEADS, SWA_HEAD_DIM))
    vw = with_prev_block(v.reshape(Bsz, N, T, SWA_KV_HEADS, SWA_HEAD_DIM))
    s = jnp.einsum('bnqkgd,bnckd->bkgnqc', qb, kw)
    qpos = jnp.arange(N)[:, None] * T + jnp.arange(T)[None, :]
    kpos = (jnp.arange(N)[:, None] - 1) * T + jnp.arange(2 * T)[None, :]
    diff = qpos[:, :, None] - kpos[:, None, :]
    mask = (diff >= 0) & (diff < SWA_WINDOW) & (kpos[:, None, :] >= 0)
    s = jnp.where(mask, s, -jnp.inf)
    sink = sinks.astype(f32).reshape(SWA_KV_HEADS, G)[None, :, :, None, None, None]
    m = jnp.maximum(jnp.max(s, axis=-1, keepdims=True), sink)
    p = jnp.exp(s - m)
    denom = jnp.sum(p, axis=-1, keepdims=True) + jnp.exp(sink - m)
    o = jnp.einsum('bkgnqc,bnckd->bkgnqd', p, vw) / denom
    o = o.transpose(0, 3, 4, 1, 2, 5).reshape(Bsz, S, SWA_HEADS * SWA_HEAD_DIM)
    return o


def setup_inputs(seed: int = 0) -> dict:
    key = jax.random.key(seed)
    ks = jax.random.split(key, 20)
    f32 = jnp.float32
    nrm = lambda k, shape, scale: jax.random.normal(k, shape, f32) * scale
    x = jax.random.normal(ks[0], (BATCH, SEQ, D_MODEL), f32)
    positions = jnp.broadcast_to(jnp.arange(SEQ, dtype=jnp.int32)[None, :], (BATCH, SEQ))
    return {
        "x": x,
        "positions": positions,
        "norm1_w": 1.0 + nrm(ks[1], (DEPTH, D_MODEL), 0.02),
        "w_in": nrm(ks[2], (DEPTH, D_MODEL, N_IN), D_MODEL ** -0.5),
        "conv_w": nrm(ks[3], (DEPTH, CONV_WIDTH, CONV_DIM), CONV_WIDTH ** -0.5),
        "conv_b": nrm(ks[4], (DEPTH, CONV_DIM), 0.02),
        "gla_wa2": nrm(ks[5], (DEPTH, GLA_RANK, GLA_HEADS * GLA_DK), GLA_RANK ** -0.5),
        "gla_ba": nrm(ks[6], (DEPTH, GLA_HEADS * GLA_DK), 0.1),
        "gla_onorm_w": 1.0 + nrm(ks[7], (DEPTH, GLA_DV), 0.02),
        "q_norm_w": 1.0 + nrm(ks[8], (DEPTH, SWA_HEAD_DIM), 0.02),
        "k_norm_w": 1.0 + nrm(ks[9], (DEPTH, SWA_HEAD_DIM), 0.02),
        "sinks": nrm(ks[10], (DEPTH, SWA_HEADS), 0.5),
        "w_branch": nrm(ks[11], (DEPTH, N_BRANCHES, BRANCH_WIDTH, D_MODEL), BRANCH_WIDTH ** -0.5),
        "w_o": nrm(ks[12], (DEPTH, D_MODEL, D_MODEL), D_MODEL ** -0.5),
        "norm2_w": 1.0 + nrm(ks[13], (DEPTH, D_MODEL), 0.02),
        "w_up": nrm(ks[14], (DEPTH, D_MODEL, D_FF), D_MODEL ** -0.5),
        "w_down": nrm(ks[15], (DEPTH, D_FF, D_MODEL), D_FF ** -0.5),
    }


def reference(x, positions, norm1_w, w_in, conv_w, conv_b, gla_wa2, gla_ba, gla_onorm_w,
              q_norm_w, k_norm_w, sinks, w_branch, w_o, norm2_w, w_up, w_down):
    Bsz, S, D = x.shape
    split_idx = np.cumsum(SPLITS)[:-1].tolist()
    for l in range(DEPTH):
        u = rms_norm(x, norm1_w[l])
        proj = u @ w_in[l]
        (cx, cb, cc, gq, gk, gv, gr, ga, sq, sk, sv, gates) = jnp.split(proj, split_idx, axis=-1)
        ya = short_conv_mixer(cx, cb, cc, conv_w[l], conv_b[l])
        yb = gla_mixer(gq, gk, gv, gr, ga, gla_wa2[l], gla_ba[l], gla_onorm_w[l])
        yc = swa_mixer(sq, sk, sv, positions, q_norm_w[l], k_norm_w[l], sinks[l])
        ybr = jnp.stack([ya.astype(x.dtype), yb.astype(x.dtype), yc.astype(x.dtype)], axis=2)
        branch = jnp.einsum('bsgc,gcd->bsgd', ybr, w_branch[l])
        gate = jax.nn.sigmoid(gates.reshape(Bsz, S, N_BRANCHES, D))
        mixed = jnp.sum(gate * branch, axis=2)
        x = x + mixed @ w_o[l]
        h = rms_norm(x, norm2_w[l])
        x = x + jnp.square(jax.nn.relu(h @ w_up[l])) @ w_down[l]
    return x
```

```python
import functools

import jax
import jax.numpy as jnp
import numpy as np
from jax import lax
from jax.experimental import pallas as pl
from jax.experimental.pallas import tpu as pltpu

F32 = jnp.float32
BF16 = jnp.bfloat16

D_MODEL = 1024
CONV_DIM = 512
CONV_WIDTH = 3
GLA_HEADS = 4
GLA_DK = 64
GLA_DV = 128
GLA_RANK = 16
GLA_TAU = 16.0
GLA_CHUNK = 64
SWA_HEADS = 8
SWA_KV_HEADS = 2
SWA_GROUP = SWA_HEADS // SWA_KV_HEADS
SWA_HEAD_DIM = 64
SWA_BLOCK = 128
ROPE_THETA = 500000.0
ROPE_DIM = SWA_HEAD_DIM // 4
N_BRANCHES = 3
BRANCH_WIDTH = 512
D_FF = 4 * D_MODEL
NORM_EPS = 1e-6

LANES = 128
SUBLANES = 8
V7X_VMEM_BYTES = 64 * 1024 * 1024

GA_PAD = LANES
C_CONV = 0
C_GQ = C_CONV + 3 * CONV_DIM
C_GK = C_GQ + GLA_HEADS * GLA_DK
C_GV = C_GK + GLA_HEADS * GLA_DK
C_GR = C_GV + GLA_HEADS * GLA_DV
C_GA = C_GR + GLA_HEADS * GLA_DV
C_SQ = C_GA + GA_PAD
C_SK = C_SQ + SWA_HEADS * SWA_HEAD_DIM
C_SV = C_SK + SWA_KV_HEADS * SWA_HEAD_DIM
C_GATE = C_SV + SWA_KV_HEADS * SWA_HEAD_DIM
N_IN_PACKED = C_GATE + N_BRANCHES * D_MODEL

TOK_TILE = 256
FFN_TILE = 512
FF_CHUNK = 1024


def _dot(a, b):
    return jnp.dot(a, b, preferred_element_type=F32)


def _dot_nt(a, b):
    return lax.dot_general(a, b, (((1,), (1,)), ((), ())), preferred_element_type=F32)


def _split_dot(m_bf16, v_f32):
    hi = v_f32.astype(BF16)
    lo = (v_f32 - hi.astype(F32)).astype(BF16)
    return _dot(m_bf16, hi) + _dot(m_bf16, lo)


def _split_dot_rhs(v_f32, m_bf16):
    hi = v_f32.astype(BF16)
    lo = (v_f32 - hi.astype(F32)).astype(BF16)
    return _dot(hi, m_bf16) + _dot(lo, m_bf16)


def _rms_scale(x):
    return x * lax.rsqrt(jnp.mean(x * x, axis=-1, keepdims=True) + NORM_EPS)


def _lane_group_mask(width, group, index):
    lane = lax.broadcasted_iota(jnp.int32, (1, width), 1)
    return jnp.where((lane // group) == index, 1.0, 0.0).astype(F32)


def _rope(x, cos, sin_lo, sin_hi):
    half = ROPE_DIM // 2
    up = pltpu.roll(x, LANES - half, axis=1)
    down = pltpu.roll(x, half, axis=1)
    return x * cos + up * sin_lo + down * sin_hi


def _mixer_kernel(sinks_ref, x_ref, pos_ref, n1w_ref, win_ref, convw_ref, convb_ref, wa2_ref, ba_ref,
                  onw_ref, qnw_ref, knw_ref, invf_ref, wbr_ref, wo_ref, out_ref,
                  conv_buf, gla_state, k_prev, v_prev, *, tok):
    t = pl.program_id(1)
    n_chunks = tok // GLA_CHUNK
    n_blocks = tok // SWA_BLOCK

    @pl.when(t == 0)
    def _():
        conv_buf[0:SUBLANES, :] = jnp.zeros((SUBLANES, CONV_DIM), F32)
        gla_state[...] = jnp.zeros_like(gla_state)
        k_prev[...] = jnp.zeros_like(k_prev)
        v_prev[...] = jnp.zeros_like(v_prev)

    x = x_ref[0]
    ub = (_rms_scale(x) * n1w_ref[...]).astype(BF16)

    pc = _dot(ub, win_ref[:, C_CONV:C_GQ])
    cx = pc[:, 0:CONV_DIM]
    cb = pc[:, CONV_DIM:2 * CONV_DIM]
    cc = pc[:, 2 * CONV_DIM:3 * CONV_DIM]
    uc = cc * cx
    conv_buf[SUBLANES:SUBLANES + tok, :] = uc
    u1 = conv_buf[SUBLANES - 1:SUBLANES - 1 + tok, :]
    u2 = conv_buf[SUBLANES - 2:SUBLANES - 2 + tok, :]
    yconv = convw_ref[0:1, :] * u2 + convw_ref[1:2, :] * u1 + convw_ref[2:3, :] * uc
    ya = (cb * (yconv + convb_ref[...])).astype(BF16)
    conv_buf[0:SUBLANES, :] = uc[tok - SUBLANES:tok, :]

    pg = _dot(ub, win_ref[:, C_GQ:C_SQ])
    o_q, o_k, o_v, o_r, o_a = 0, C_GK - C_GQ, C_GV - C_GQ, C_GR - C_GQ, C_GA - C_GQ
    gq = pg[:, o_q:o_k] * (GLA_DK ** -0.5)
    gk = pg[:, o_k:o_v]
    gv = pg[:, o_v:o_r]
    gr = pg[:, o_r:o_a]
    ga = pg[:, o_a:o_a + GA_PAD]
    z = _dot(ga.astype(BF16), wa2_ref[...]) + ba_ref[...]
    glog = (jnp.minimum(z, 0.0) - jnp.log1p(jnp.exp(-jnp.abs(z)))) * (1.0 / GLA_TAU)

    ri = lax.broadcasted_iota(jnp.int32, (tok, tok), 0)
    ci = lax.broadcasted_iota(jnp.int32, (tok, tok), 1)
    same_chunk = (ri // GLA_CHUNK) == (ci // GLA_CHUNK)
    causal_chunk = same_chunk & (ci <= ri)
    l_tri = jnp.where(causal_chunk, 1.0, 0.0).astype(BF16)
    l_all = jnp.where(same_chunk, 1.0, 0.0).astype(BF16)
    b = _split_dot(l_tri, glog)
    b_last = _split_dot(l_all, glog)
    q_in = gq * jnp.exp(b)
    k_in = (gk * jnp.exp(-b)).astype(BF16)
    k_end = gk * jnp.exp(b_last - b)
    decay = jnp.exp(b_last)

    head_masks = [_lane_group_mask(GLA_HEADS * GLA_DK, GLA_DK, h) for h in range(GLA_HEADS)]
    gvb = gv.astype(BF16)

    state = gla_state[...]
    o_inter = [[] for _ in range(GLA_HEADS)]
    for c in range(n_chunks):
        r0 = c * GLA_CHUNK
        q_c = q_in[r0:r0 + GLA_CHUNK, :]
        ke_c = k_end[r0:r0 + GLA_CHUNK, :]
        q_stack = jnp.concatenate([q_c * head_masks[h] for h in range(GLA_HEADS)], axis=0).astype(BF16)
        oi = _dot_nt(q_stack, state.astype(BF16))
        for h in range(GLA_HEADS):
            o_inter[h].append(oi[h * GLA_CHUNK:(h + 1) * GLA_CHUNK, :])
        k_stack = jnp.concatenate([ke_c * head_masks[h] for h in range(GLA_HEADS)], axis=0).astype(BF16)
        v_stack = jnp.concatenate(
            [gv[r0:r0 + GLA_CHUNK, h * GLA_DV:(h + 1) * GLA_DV] for h in range(GLA_HEADS)], axis=0)
        upd = _dot(v_stack.T.astype(BF16), k_stack)
        state = decay[r0:r0 + 1, :] * state + upd
    gla_state[...] = state

    yb_parts = []
    for h in range(GLA_HEADS):
        qh = (q_in * head_masks[h]).astype(BF16)
        att = _dot_nt(qh, k_in)
        att = jnp.where(causal_chunk, att, 0.0).astype(BF16)
        o_h = _dot(att, gvb[:, h * GLA_DV:(h + 1) * GLA_DV]) + jnp.concatenate(o_inter[h], axis=0)
        o_h = _rms_scale(o_h) * onw_ref[...]
        r_h = gr[:, h * GLA_DV:(h + 1) * GLA_DV]
        yb_parts.append((o_h * (r_h * jax.nn.sigmoid(r_h))).astype(BF16))
    yb = jnp.concatenate(yb_parts, axis=1)

    ps = _dot(ub, win_ref[:, C_SQ:C_GATE])
    sq = ps[:, 0:C_SK - C_SQ]
    sk = ps[:, C_SK - C_SQ:C_SV - C_SQ]
    sv = ps[:, C_SV - C_SQ:C_GATE - C_SQ]

    gi = lax.broadcasted_iota(jnp.int32, (2 * LANES, 2 * LANES), 0) // SWA_HEAD_DIM
    gj = lax.broadcasted_iota(jnp.int32, (2 * LANES, 2 * LANES), 1) // SWA_HEAD_DIM
    head_ones = jnp.where(gi == gj, 1.0, 0.0).astype(BF16)

    ang = pos_ref[0].astype(F32) * invf_ref[...]
    cos = jnp.cos(ang)
    sin = jnp.sin(ang)
    lane_in_head = lax.broadcasted_iota(jnp.int32, (1, LANES), 1) % SWA_HEAD_DIM
    sin_lo = jnp.where(lane_in_head < ROPE_DIM // 2, -sin, 0.0)
    sin_hi = jnp.where((lane_in_head >= ROPE_DIM // 2) & (lane_in_head < ROPE_DIM), sin, 0.0)

    q_cols = []
    for j in range(SWA_HEADS * SWA_HEAD_DIM // (2 * LANES)):
        qj = sq[:, j * 2 * LANES:(j + 1) * 2 * LANES]
        ssq = _split_dot_rhs(qj * qj, head_ones)
        qn = qj * lax.rsqrt(ssq * (1.0 / SWA_HEAD_DIM) + NORM_EPS)
        for i in range(2):
            qn_i = qn[:, i * LANES:(i + 1) * LANES] * qnw_ref[...]
            q_cols.append(_rope(qn_i, cos, sin_lo, sin_hi) * (SWA_HEAD_DIM ** -0.5))
    ssk = _split_dot_rhs(sk * sk, head_ones[0:LANES, 0:LANES])
    kn = sk * lax.rsqrt(ssk * (1.0 / SWA_HEAD_DIM) + NORM_EPS) * knw_ref[...]
    kr = _rope(kn, cos, sin_lo, sin_hi)

    group_masks = [_lane_group_mask(2 * LANES, SWA_HEAD_DIM, g) for g in range(SWA_GROUP)]
    lane128 = lax.broadcasted_iota(jnp.int32, (1, LANES), 1)
    rows = lax.broadcasted_iota(jnp.int32, (SWA_GROUP * SWA_BLOCK, 2 * SWA_BLOCK), 0) % SWA_BLOCK
    cols = lax.broadcasted_iota(jnp.int32, (SWA_GROUP * SWA_BLOCK, 2 * SWA_BLOCK), 1)
    band = (cols > rows) & (cols <= rows + SWA_BLOCK)
    first_cols = jnp.where(t == 0, SWA_BLOCK, 0)

    yc_blocks = []
    for n in range(n_blocks):
        r0 = n * SWA_BLOCK
        if n == 0:
            k_win = jnp.concatenate([k_prev[...], kr[0:SWA_BLOCK, :]], axis=0)
            v_win = jnp.concatenate([v_prev[...], sv[0:SWA_BLOCK, :]], axis=0)
            mask = band & (cols >= first_cols)
        else:
            k_win = kr[r0 - SWA_BLOCK:r0 + SWA_BLOCK, :]
            v_win = sv[r0 - SWA_BLOCK:r0 + SWA_BLOCK, :]
            mask = band
        k_sw = pltpu.roll(k_win, SWA_HEAD_DIM, axis=1)
        v_sw = pltpu.roll(v_win, SWA_HEAD_DIM, axis=1)
        yc_kv = []
        for kv in range(SWA_KV_HEADS):
            own = (lane128 // SWA_HEAD_DIM) == kv
            k_one = jnp.where(own, k_win, k_sw).astype(BF16)
            v_one = jnp.where(own, v_win, v_sw).astype(BF16)
            k_rep = jnp.concatenate([k_one, k_one], axis=1)
            v_rep = jnp.concatenate([v_one, v_one], axis=1)
            q_kv = jnp.concatenate(q_cols[2 * kv:2 * kv + 2], axis=1)[r0:r0 + SWA_BLOCK, :]
            q_stack = jnp.concatenate([q_kv * group_masks[g] for g in range(SWA_GROUP)], axis=0).astype(BF16)
            s = _dot_nt(q_stack, k_rep)
            s = jnp.where(mask, s, -jnp.inf)
            sink = jnp.concatenate(
                [jnp.full((SWA_BLOCK, 1), sinks_ref[kv * SWA_GROUP + g], F32) for g in range(SWA_GROUP)], axis=0)
            m = jnp.maximum(jnp.max(s, axis=-1, keepdims=True), sink)
            p = jnp.exp(s - m)
            denom = jnp.sum(p, axis=-1, keepdims=True) + jnp.exp(sink - m)
            o = _dot(p.astype(BF16), v_rep) / denom
            y = o[0:SWA_BLOCK, :] * group_masks[0]
            for g in range(1, SWA_GROUP):
                y = y + o[g * SWA_BLOCK:(g + 1) * SWA_BLOCK, :] * group_masks[g]
            yc_kv.append(y)
        yc_blocks.append(jnp.concatenate(yc_kv, axis=1))
    yc = jnp.concatenate(yc_blocks, axis=0).astype(BF16)
    k_prev[...] = kr[tok - SWA_BLOCK:tok, :]
    v_prev[...] = sv[tok - SWA_BLOCK:tok, :]

    mixed = None
    for g, y_g in enumerate((ya, yb, yc)):
        gate = jax.nn.sigmoid(_dot(ub, win_ref[:, C_GATE + g * D_MODEL:C_GATE + (g + 1) * D_MODEL]))
        term = gate * _dot(y_g, wbr_ref[g])
        mixed = term if mixed is None else mixed + term
    out_ref[0] = x + _dot(mixed.astype(BF16), wo_ref[...])


def _ffn_kernel(x_ref, n2w_ref, wup_ref, wdown_ref, out_ref):
    x = x_ref[...]
    hb = (_rms_scale(x) * n2w_ref[...]).astype(BF16)
    acc = x
    for c in range(D_FF // FF_CHUNK):
        a = jnp.maximum(_dot(hb, wup_ref[:, c * FF_CHUNK:(c + 1) * FF_CHUNK]), 0.0)
        acc = acc + _dot((a * a).astype(BF16), wdown_ref[c * FF_CHUNK:(c + 1) * FF_CHUNK, :])
    out_ref[...] = acc


def _const_spec(shape):
    zeros = (0,) * len(shape)
    return pl.BlockSpec(shape, lambda *_: zeros, pipeline_mode=pl.Buffered(1))


def _mixer_call(x, pos3, p, tok):
    bsz, seq, d = x.shape
    grid = (bsz, seq // tok)
    tile = lambda b, t: (b, t, 0)
    in_specs = [
        pl.BlockSpec(memory_space=pltpu.SMEM),
        pl.BlockSpec((1, tok, d), tile),
        pl.BlockSpec((1, tok, 1), tile),
        _const_spec((1, d)),
        _const_spec((d, N_IN_PACKED)),
        _const_spec((CONV_WIDTH, CONV_DIM)),
        _const_spec((1, CONV_DIM)),
        _const_spec((GA_PAD, GLA_HEADS * GLA_DK)),
        _const_spec((1, GLA_HEADS * GLA_DK)),
        _const_spec((1, GLA_DV)),
        _const_spec((1, LANES)),
        _const_spec((1, LANES)),
        _const_spec((1, LANES)),
        _const_spec((N_BRANCHES, BRANCH_WIDTH, d)),
        _const_spec((d, d)),
    ]
    scratch = [
        pltpu.VMEM((tok + SUBLANES, CONV_DIM), F32),
        pltpu.VMEM((GLA_DV, GLA_HEADS * GLA_DK), F32),
        pltpu.VMEM((SWA_BLOCK, SWA_KV_HEADS * SWA_HEAD_DIM), F32),
        pltpu.VMEM((SWA_BLOCK, SWA_KV_HEADS * SWA_HEAD_DIM), F32),
    ]
    return pl.pallas_call(
        functools.partial(_mixer_kernel, tok=tok),
        grid=grid,
        in_specs=in_specs,
        out_specs=pl.BlockSpec((1, tok, d), tile),
        out_shape=jax.ShapeDtypeStruct(x.shape, F32),
        scratch_shapes=scratch,
        compiler_params=pltpu.CompilerParams(
            dimension_semantics=("arbitrary", "arbitrary"),
            vmem_limit_bytes=V7X_VMEM_BYTES * 3 // 4),
        name="mixer",
    )(p["sinks"], x, pos3, p["n1w"], p["w_in"], p["conv_w"], p["conv_b"], p["wa2"], p["ba"],
      p["onw"], p["qnw"], p["knw"], p["invf"], p["w_branch"], p["w_o"])


def _ffn_call(x2, p, tile):
    n, d = x2.shape
    return pl.pallas_call(
        _ffn_kernel,
        grid=(n // tile,),
        in_specs=[
            pl.BlockSpec((tile, d), lambda i: (i, 0)),
            _const_spec((1, d)),
            _const_spec((d, D_FF)),
            _const_spec((D_FF, d)),
        ],
        out_specs=pl.BlockSpec((tile, d), lambda i: (i, 0)),
        out_shape=jax.ShapeDtypeStruct(x2.shape, F32),
        compiler_params=pltpu.CompilerParams(
            dimension_semantics=("arbitrary",),
            vmem_limit_bytes=V7X_VMEM_BYTES * 3 // 4),
        name="ffn",
    )(x2, p["n2w"], p["w_up"], p["w_down"])


def _rope_lane_frequencies():
    inv_freq = ROPE_THETA ** (-jnp.arange(0, ROPE_DIM, 2, dtype=F32) / ROPE_DIM)
    per_head = jnp.concatenate([inv_freq, inv_freq, jnp.zeros((SWA_HEAD_DIM - ROPE_DIM,), F32)])
    return jnp.tile(per_head, LANES // SWA_HEAD_DIM)[None, :]


def _layer_params(l, norm1_w, w_in, conv_w, conv_b, gla_wa2, gla_ba, gla_onorm_w, q_norm_w, k_norm_w,
                  sinks, w_branch, w_o, norm2_w, w_up, w_down):
    w = w_in[l]
    ga0 = 3 * CONV_DIM + 2 * GLA_HEADS * GLA_DK + 2 * GLA_HEADS * GLA_DV
    ga_cols = jnp.pad(w[:, ga0:ga0 + GLA_RANK], ((0, 0), (0, GA_PAD - GLA_RANK)))
    w_packed = jnp.concatenate([w[:, :ga0], ga_cols, w[:, ga0 + GLA_RANK:]], axis=1).astype(BF16)
    return {
        "sinks": sinks[l].astype(F32),
        "n1w": norm1_w[l][None, :],
        "w_in": w_packed,
        "conv_w": conv_w[l],
        "conv_b": conv_b[l][None, :],
        "wa2": jnp.pad(gla_wa2[l], ((0, GA_PAD - GLA_RANK), (0, 0))).astype(BF16),
        "ba": gla_ba[l][None, :],
        "onw": gla_onorm_w[l][None, :],
        "qnw": jnp.tile(q_norm_w[l], LANES // SWA_HEAD_DIM)[None, :],
        "knw": jnp.tile(k_norm_w[l], LANES // SWA_HEAD_DIM)[None, :],
        "invf": _rope_lane_frequencies(),
        "w_branch": w_branch[l].astype(BF16),
        "w_o": w_o[l].astype(BF16),
        "n2w": norm2_w[l][None, :],
        "w_up": w_up[l].astype(BF16),
        "w_down": w_down[l].astype(BF16),
    }


def kernel(x, positions, norm1_w, w_in, conv_w, conv_b, gla_wa2, gla_ba, gla_onorm_w, q_norm_w, k_norm_w,
           sinks, w_branch, w_o, norm2_w, w_up, w_down):
    bsz, seq, d = x.shape
    assert d == D_MODEL and w_in.shape[2] == N_IN_PACKED - GA_PAD + GLA_RANK
    tok = min(TOK_TILE, seq)
    assert seq % tok == 0 and tok % (2 * SWA_BLOCK) == 0
    ffn_tile = min(FFN_TILE, bsz * seq)
    assert (bsz * seq) % ffn_tile == 0
    pos3 = positions.astype(jnp.int32)[:, :, None]
    for l in range(w_in.shape[0]):
        p = _layer_params(l, norm1_w, w_in, conv_w, conv_b, gla_wa2, gla_ba, gla_onorm_w, q_norm_w,
                          k_norm_w, sinks, w_branch, w_o, norm2_w, w_up, w_down)
        x = _mixer_call(x, pos3, p, tok)
        x = _ffn_call(x.reshape(bsz * seq, d), p, ffn_tile).reshape(bsz, seq, d)
    return x
```

```python
import functools

import jax
import jax.numpy as jnp
import numpy as np
from jax import lax
from jax.experimental import pallas as pl
from jax.experimental.pallas import tpu as pltpu

F32 = jnp.float32
BF16 = jnp.bfloat16

D_MODEL = 1024
CONV_DIM = 512
CONV_WIDTH = 3
GLA_HEADS = 4
GLA_DK = 64
GLA_DV = 128
GLA_RANK = 16
GLA_TAU = 16.0
GLA_CHUNK = 64
SWA_HEADS = 8
SWA_KV_HEADS = 2
SWA_GROUP = SWA_HEADS // SWA_KV_HEADS
SWA_HEAD_DIM = 64
SWA_BLOCK = 128
ROPE_THETA = 500000.0
ROPE_DIM = SWA_HEAD_DIM // 4
N_BRANCHES = 3
BRANCH_WIDTH = 512
D_FF = 4 * D_MODEL
NORM_EPS = 1e-6
LOG2_E = 1.4426950408889634

LANES = 128
SUBLANES = 8
V7X_VMEM_BYTES = 64 * 1024 * 1024

GA_PAD = LANES
C_CONV = 0
C_GQ = C_CONV + 3 * CONV_DIM
C_GK = C_GQ + GLA_HEADS * GLA_DK
C_GV = C_GK + GLA_HEADS * GLA_DK
C_GR = C_GV + GLA_HEADS * GLA_DV
C_GA = C_GR + GLA_HEADS * GLA_DV
C_SQ = C_GA + GA_PAD
C_SK = C_SQ + SWA_HEADS * SWA_HEAD_DIM
C_SV = C_SK + SWA_KV_HEADS * SWA_HEAD_DIM
C_GATE = C_SV + SWA_KV_HEADS * SWA_HEAD_DIM
N_IN_PACKED = C_GATE + N_BRANCHES * D_MODEL

ROPE_TOKENS_PER_ROW = LANES // (ROPE_DIM // 2)
ROPE_TILE = 1024
TOK_TILE = 256
FFN_TILE = 512
FF_CHUNK = 1024


def _dot(a, b):
    return jnp.dot(a, b, preferred_element_type=F32)


def _dot_nt(a, b):
    return lax.dot_general(a, b, (((1,), (1,)), ((), ())), preferred_element_type=F32)


def _split_dot(m_bf16, v_f32):
    hi = v_f32.astype(BF16)
    lo = (v_f32 - hi.astype(F32)).astype(BF16)
    return _dot(m_bf16, hi) + _dot(m_bf16, lo)


def _split_dot_rhs(v_f32, m_bf16):
    hi = v_f32.astype(BF16)
    lo = (v_f32 - hi.astype(F32)).astype(BF16)
    return _dot(hi, m_bf16) + _dot(lo, m_bf16)


def _rms_scale(x):
    return x * lax.rsqrt(jnp.mean(x * x, axis=-1, keepdims=True) + NORM_EPS)


def _lane_group_mask(width, group, index):
    lane = lax.broadcasted_iota(jnp.int32, (1, width), 1)
    return jnp.where((lane // group) == index, 1.0, 0.0).astype(F32)


def _split3(v_f32):
    hi = v_f32.astype(BF16)
    r1 = v_f32 - hi.astype(F32)
    mid = r1.astype(BF16)
    lo = (r1 - mid.astype(F32)).astype(BF16)
    return hi, mid, lo


def _rope_table_kernel(pos_ref, invf_ref, spread_ref, cos_ref, sin_ref, *, tokens):
    rows = tokens // ROPE_TOKENS_PER_ROW
    ang = pos_ref[...].astype(F32) * invf_ref[...]
    dense = jnp.concatenate([jnp.cos(ang), jnp.sin(ang)], axis=1)
    tok_r = lax.broadcasted_iota(jnp.int32, (tokens, rows), 0) // ROPE_TOKENS_PER_ROW
    row_r = lax.broadcasted_iota(jnp.int32, (tokens, rows), 1)
    pick_row = jnp.where(tok_r == row_r, 1.0, 0.0).astype(BF16)
    by_tok = sum(_dot(pick_row, piece) for piece in _split3(dense))
    tok_c = lax.broadcasted_iota(jnp.int32, (tokens, 2 * LANES), 0) % ROPE_TOKENS_PER_ROW
    slot_c = (lax.broadcasted_iota(jnp.int32, (tokens, 2 * LANES), 1) % LANES) // (ROPE_DIM // 2)
    own = jnp.where(tok_c == slot_c, by_tok, 0.0)
    out = sum(_dot(piece, spread_ref[...]) for piece in _split3(own))
    lane_in_head = lax.broadcasted_iota(jnp.int32, (1, LANES), 1) % SWA_HEAD_DIM
    cos_ref[...] = out[:, 0:LANES] + jnp.where(lane_in_head >= ROPE_DIM, 1.0, 0.0)
    sin_ref[...] = out[:, LANES:2 * LANES]


def _rope(x, cos, sin_signed, lower_half):
    half = ROPE_DIM // 2
    up = pltpu.roll(x, LANES - half, axis=1)
    down = pltpu.roll(x, half, axis=1)
    return x * cos + jnp.where(lower_half, up, down) * sin_signed


def _mixer_kernel(sinks_ref, x_ref, cos_ref, sin_ref, n1w_ref, win_ref, convw_ref, convb_ref, wa2_ref, ba_ref,
                  onw_ref, qnw_ref, knw_ref, wbr_ref, wo_ref, out_ref,
                  conv_buf, gla_state, k_prev, v_prev, *, tok):
    t = pl.program_id(1)
    n_chunks = tok // GLA_CHUNK
    n_blocks = tok // SWA_BLOCK

    @pl.when(t == 0)
    def _():
        conv_buf[0:SUBLANES, :] = jnp.zeros((SUBLANES, CONV_DIM), F32)
        gla_state[...] = jnp.zeros_like(gla_state)
        k_prev[...] = jnp.zeros_like(k_prev)
        v_prev[...] = jnp.zeros_like(v_prev)

    x = x_ref[0]
    ub = (_rms_scale(x) * n1w_ref[...]).astype(BF16)

    gate_cols = 2 * LANES
    gate_pieces = []

    def next_gate_piece():
        c0 = C_GATE + len(gate_pieces) * gate_cols
        gate_pieces.append(jax.nn.sigmoid(_dot(ub, win_ref[:, c0:c0 + gate_cols])))


    pg = _dot(ub, win_ref[:, C_GQ:C_SQ])
    o_q, o_k, o_v, o_r, o_a = 0, C_GK - C_GQ, C_GV - C_GQ, C_GR - C_GQ, C_GA - C_GQ
    gq = pg[:, o_q:o_k] * (GLA_DK ** -0.5)
    gk = pg[:, o_k:o_v]
    gv = pg[:, o_v:o_r]
    gr = pg[:, o_r:o_a]
    ga = pg[:, o_a:o_a + GA_PAD]
    z = _dot(ga.astype(BF16), wa2_ref[...]) + ba_ref[...]
    glog = (jnp.minimum(z, 0.0) - jnp.log(1.0 + jnp.exp(-jnp.abs(z)))) * (1.0 / GLA_TAU)

    pc = _dot(ub, win_ref[:, C_CONV:C_GQ])
    cx = pc[:, 0:CONV_DIM]
    cb = pc[:, CONV_DIM:2 * CONV_DIM]
    cc = pc[:, 2 * CONV_DIM:3 * CONV_DIM]
    uc = cc * cx
    conv_buf[SUBLANES:SUBLANES + tok, :] = uc
    u1 = conv_buf[SUBLANES - 1:SUBLANES - 1 + tok, :]
    u2 = conv_buf[SUBLANES - 2:SUBLANES - 2 + tok, :]
    yconv = convw_ref[0:1, :] * u2 + convw_ref[1:2, :] * u1 + convw_ref[2:3, :] * uc
    ya = (cb * (yconv + convb_ref[...])).astype(BF16)
    conv_buf[0:SUBLANES, :] = uc[tok - SUBLANES:tok, :]

    ri = lax.broadcasted_iota(jnp.int32, (tok, tok), 0)
    ci = lax.broadcasted_iota(jnp.int32, (tok, tok), 1)
    same_chunk = (ri // GLA_CHUNK) == (ci // GLA_CHUNK)
    causal_chunk = same_chunk & (ci <= ri)
    l_tri = jnp.where(causal_chunk, 1.0, 0.0).astype(BF16)
    l_all = jnp.where(same_chunk, 1.0, 0.0).astype(BF16)
    b = _split_dot(l_tri, glog)
    b_last = _split_dot(l_all, glog)
    q_in = gq * jnp.exp(b)
    k_in = (gk * jnp.exp(-b)).astype(BF16)
    k_end = gk * jnp.exp(b_last - b)
    decay = jnp.exp(b_last)

    br_a = _dot(ya, wbr_ref[0])

    ps = _dot(ub, win_ref[:, C_SQ:C_GATE])
    sq = ps[:, 0:C_SK - C_SQ]
    sk = ps[:, C_SK - C_SQ:C_SV - C_SQ]
    sv = ps[:, C_SV - C_SQ:C_GATE - C_SQ]

    gi = lax.broadcasted_iota(jnp.int32, (2 * LANES, 2 * LANES), 0) // SWA_HEAD_DIM
    gj = lax.broadcasted_iota(jnp.int32, (2 * LANES, 2 * LANES), 1) // SWA_HEAD_DIM
    head_ones = jnp.where(gi == gj, 1.0, 0.0).astype(BF16)

    cos = cos_ref[0]
    sin_signed = sin_ref[0]
    lower_half = (lax.broadcasted_iota(jnp.int32, (1, LANES), 1) % SWA_HEAD_DIM) < ROPE_DIM // 2

    q_cols = []
    for j in range(SWA_HEADS * SWA_HEAD_DIM // (2 * LANES)):
        qj = sq[:, j * 2 * LANES:(j + 1) * 2 * LANES]
        ssq = _split_dot_rhs(qj * qj, head_ones)
        qn = qj * lax.rsqrt(ssq * (1.0 / SWA_HEAD_DIM) + NORM_EPS)
        for i in range(2):
            qn_i = qn[:, i * LANES:(i + 1) * LANES] * qnw_ref[...]
            q_cols.append((_rope(qn_i, cos, sin_signed, lower_half) * (SWA_HEAD_DIM ** -0.5 * LOG2_E)).astype(BF16))
    ssk = _split_dot_rhs(sk * sk, head_ones[0:LANES, 0:LANES])
    kn = sk * lax.rsqrt(ssk * (1.0 / SWA_HEAD_DIM) + NORM_EPS) * knw_ref[...]
    kr = _rope(kn, cos, sin_signed, lower_half)

    head_masks = [_lane_group_mask(GLA_HEADS * GLA_DK, GLA_DK, h).astype(BF16) for h in range(GLA_HEADS)]
    gvb = gv.astype(BF16)
    q_in_b = q_in.astype(BF16)
    k_end_b = k_end.astype(BF16)
    state = gla_state[...]
    o_inter = [[] for _ in range(GLA_HEADS)]
    for c in range(n_chunks):
        r0 = c * GLA_CHUNK
        q_c = q_in_b[r0:r0 + GLA_CHUNK, :]
        ke_c = k_end_b[r0:r0 + GLA_CHUNK, :]
        q_stack = jnp.concatenate([q_c * head_masks[h] for h in range(GLA_HEADS)], axis=0)
        oi = _dot_nt(q_stack, state.astype(BF16))
        for h in range(GLA_HEADS):
            o_inter[h].append(oi[h * GLA_CHUNK:(h + 1) * GLA_CHUNK, :])
        k_stack = jnp.concatenate([ke_c * head_masks[h] for h in range(GLA_HEADS)], axis=0)
        v_stack = jnp.concatenate(
            [gv[r0:r0 + GLA_CHUNK, h * GLA_DV:(h + 1) * GLA_DV] for h in range(GLA_HEADS)], axis=0)
        upd = _dot(v_stack.T.astype(BF16), k_stack)
        state = decay[r0:r0 + 1, :] * state + upd
        next_gate_piece()
    gla_state[...] = state

    yb_parts = []
    for h in range(GLA_HEADS):
        att = _dot_nt(q_in_b * head_masks[h], k_in)
        att = jnp.where(causal_chunk, att, 0.0).astype(BF16)
        o_h = _dot(att, gvb[:, h * GLA_DV:(h + 1) * GLA_DV]) + jnp.concatenate(o_inter[h], axis=0)
        o_h = _rms_scale(o_h) * onw_ref[...]
        r_h = gr[:, h * GLA_DV:(h + 1) * GLA_DV]
        yb_parts.append((o_h * (r_h * jax.nn.sigmoid(r_h))).astype(BF16))
    yb = jnp.concatenate(yb_parts, axis=1)

    br_b = _dot(yb, wbr_ref[1])

    group_masks = [_lane_group_mask(2 * LANES, SWA_HEAD_DIM, g).astype(BF16) for g in range(SWA_GROUP)]
    lane128 = lax.broadcasted_iota(jnp.int32, (1, LANES), 1)
    slot_j = lax.broadcasted_iota(jnp.int32, (SWA_BLOCK, SWA_GROUP * SWA_BLOCK), 0)
    qry_i = lax.broadcasted_iota(jnp.int32, (SWA_BLOCK, SWA_GROUP * SWA_BLOCK), 1) % SWA_BLOCK
    from_prev = slot_j > qry_i
    no_prev_bias = jnp.where(t == 0, -jnp.inf, 0.0)

    def kv_keys(k_blk):
        k_sw = pltpu.roll(k_blk, SWA_HEAD_DIM, axis=1)
        reps = []
        for kv in range(SWA_KV_HEADS):
            k_one = jnp.where((lane128 // SWA_HEAD_DIM) == kv, k_blk, k_sw).astype(BF16)
            reps.append(jnp.concatenate([k_one, k_one], axis=1))
        return reps

    keys_prev = kv_keys(k_prev[...])
    vt_prev = v_prev[...].astype(BF16)
    yc_blocks = []
    for n in range(n_blocks):
        r0 = n * SWA_BLOCK
        keys_cur = kv_keys(kr[r0:r0 + SWA_BLOCK, :])
        vt_cur_f32 = sv[r0:r0 + SWA_BLOCK, :].T
        vt_cur = vt_cur_f32.astype(BF16)
        o_t_heads = []
        for kv in range(SWA_KV_HEADS):
            q_kv = jnp.concatenate(q_cols[2 * kv:2 * kv + 2], axis=1)[r0:r0 + SWA_BLOCK, :]
            q_stack = jnp.concatenate([q_kv * group_masks[g] for g in range(SWA_GROUP)], axis=0)
            s_prev = _dot_nt(keys_prev[kv], q_stack)
            s_cur = _dot_nt(keys_cur[kv], q_stack)
            if n == 0:
                s_prev = s_prev + no_prev_bias
            s_t = jnp.where(from_prev, s_prev, s_cur)
            sink = jnp.concatenate(
                [jnp.full((1, SWA_BLOCK), sinks_ref[kv * SWA_GROUP + g] * LOG2_E, F32) for g in range(SWA_GROUP)],
                axis=1)
            m = jnp.maximum(jnp.max(s_t, axis=0, keepdims=True), sink)
            p_t = jnp.exp2(s_t - m)
            denom = jnp.sum(p_t, axis=0, keepdims=True) + jnp.exp2(sink - m)
            hd = slice(kv * SWA_HEAD_DIM, (kv + 1) * SWA_HEAD_DIM)
            o_t = (_dot(vt_prev[hd, :], jnp.where(from_prev, p_t, 0.0).astype(BF16))
                   + _dot(vt_cur[hd, :], jnp.where(from_prev, 0.0, p_t).astype(BF16)))
            o_t = o_t * (1.0 / denom)
            for g in range(SWA_GROUP):
                o_t_heads.append(o_t[:, g * SWA_BLOCK:(g + 1) * SWA_BLOCK])
            next_gate_piece()
            next_gate_piece()
        yc_blocks.append(jnp.concatenate(o_t_heads, axis=0).T)
        keys_prev, vt_prev = keys_cur, vt_cur
    yc = jnp.concatenate(yc_blocks, axis=0).astype(BF16)
    k_prev[...] = kr[tok - SWA_BLOCK:tok, :]
    v_prev[...] = vt_cur_f32

    br_c = _dot(yc, wbr_ref[2])
    while len(gate_pieces) < N_BRANCHES * D_MODEL // gate_cols:
        next_gate_piece()
    per_branch = D_MODEL // gate_cols
    mixed = None
    for g, br in enumerate((br_a, br_b, br_c)):
        gate = jnp.concatenate(gate_pieces[g * per_branch:(g + 1) * per_branch], axis=1)
        mixed = gate * br if mixed is None else mixed + gate * br
    out_ref[0] = x + _dot(mixed.astype(BF16), wo_ref[...])


def _ffn_kernel(x_ref, n2w_ref, wup_ref, wdown_ref, out_ref):
    x = x_ref[...]
    hb = (_rms_scale(x) * n2w_ref[...]).astype(BF16)
    acc = x
    for c in range(D_FF // FF_CHUNK):
        a = jnp.maximum(_dot(hb, wup_ref[:, c * FF_CHUNK:(c + 1) * FF_CHUNK]), 0.0)
        acc = acc + _dot((a * a).astype(BF16), wdown_ref[c * FF_CHUNK:(c + 1) * FF_CHUNK, :])
    out_ref[...] = acc


def _const_spec(shape):
    zeros = (0,) * len(shape)
    return pl.BlockSpec(shape, lambda *_: zeros, pipeline_mode=pl.Buffered(1))


def _mixer_call(x, cos, sin, p, tok):
    bsz, seq, d = x.shape
    grid = (bsz, seq // tok)
    tile = lambda b, t: (b, t, 0)
    in_specs = [
        pl.BlockSpec(memory_space=pltpu.SMEM),
        pl.BlockSpec((1, tok, d), tile),
        pl.BlockSpec((1, tok, LANES), tile),
        pl.BlockSpec((1, tok, LANES), tile),
        _const_spec((1, d)),
        _const_spec((d, N_IN_PACKED)),
        _const_spec((CONV_WIDTH, CONV_DIM)),
        _const_spec((1, CONV_DIM)),
        _const_spec((GA_PAD, GLA_HEADS * GLA_DK)),
        _const_spec((1, GLA_HEADS * GLA_DK)),
        _const_spec((1, GLA_DV)),
        _const_spec((1, LANES)),
        _const_spec((1, LANES)),
        _const_spec((N_BRANCHES, BRANCH_WIDTH, d)),
        _const_spec((d, d)),
    ]
    scratch = [
        pltpu.VMEM((tok + SUBLANES, CONV_DIM), F32),
        pltpu.VMEM((GLA_DV, GLA_HEADS * GLA_DK), F32),
        pltpu.VMEM((SWA_BLOCK, SWA_KV_HEADS * SWA_HEAD_DIM), F32),
        pltpu.VMEM((SWA_BLOCK, SWA_KV_HEADS * SWA_HEAD_DIM), F32),
    ]
    return pl.pallas_call(
        functools.partial(_mixer_kernel, tok=tok),
        grid=grid,
        in_specs=in_specs,
        out_specs=pl.BlockSpec((1, tok, d), tile),
        out_shape=jax.ShapeDtypeStruct(x.shape, F32),
        scratch_shapes=scratch,
        compiler_params=pltpu.CompilerParams(
            dimension_semantics=("arbitrary", "arbitrary"),
            vmem_limit_bytes=V7X_VMEM_BYTES * 3 // 4),
        name="mixer",
    )(p["sinks"], x, cos, sin, p["n1w"], p["w_in"], p["conv_w"], p["conv_b"], p["wa2"], p["ba"],
      p["onw"], p["qnw"], p["knw"], p["w_branch"], p["w_o"])


def _ffn_call(x2, p, tile):
    n, d = x2.shape
    return pl.pallas_call(
        _ffn_kernel,
        grid=(n // tile,),
        in_specs=[
            pl.BlockSpec((tile, d), lambda i: (i, 0)),
            _const_spec((1, d)),
            _const_spec((d, D_FF)),
            _const_spec((D_FF, d)),
        ],
        out_specs=pl.BlockSpec((tile, d), lambda i: (i, 0)),
        out_shape=jax.ShapeDtypeStruct(x2.shape, F32),
        compiler_params=pltpu.CompilerParams(
            dimension_semantics=("arbitrary",),
            vmem_limit_bytes=V7X_VMEM_BYTES * 3 // 4),
        name="ffn",
    )(x2, p["n2w"], p["w_up"], p["w_down"])


def _rope_spread_matrix():
    half = ROPE_DIM // 2
    c = np.arange(LANES)[:, None]
    l = np.arange(LANES)[None, :]
    hit = ((c % half) == (l % half)) & ((l % SWA_HEAD_DIM) < ROPE_DIM)
    sign = np.where((l % SWA_HEAD_DIM) < half, -1.0, 1.0)
    m = np.zeros((2 * LANES, 2 * LANES), np.float32)
    m[:LANES, :LANES] = hit
    m[LANES:, LANES:] = hit * sign
    return jnp.asarray(m, BF16)


def _rope_tables(positions):
    bsz, seq = positions.shape
    n = bsz * seq
    tokens = min(ROPE_TILE, n)
    assert n % tokens == 0 and tokens % (ROPE_TOKENS_PER_ROW * SUBLANES) == 0
    half = ROPE_DIM // 2
    inv_freq = ROPE_THETA ** (-jnp.arange(0, ROPE_DIM, 2, dtype=F32) / ROPE_DIM)
    invf_dense = jnp.tile(inv_freq, LANES // half)[None, :]
    pos_dense = jnp.repeat(positions.astype(jnp.int32).reshape(n), half).reshape(n // ROPE_TOKENS_PER_ROW, LANES)
    rows = tokens // ROPE_TOKENS_PER_ROW
    cos, sin = pl.pallas_call(
        functools.partial(_rope_table_kernel, tokens=tokens),
        grid=(n // tokens,),
        in_specs=[
            pl.BlockSpec((rows, LANES), lambda i: (i, 0)),
            _const_spec((1, LANES)),
            _const_spec((2 * LANES, 2 * LANES)),
        ],
        out_specs=[pl.BlockSpec((tokens, LANES), lambda i: (i, 0))] * 2,
        out_shape=[jax.ShapeDtypeStruct((n, LANES), F32)] * 2,
        compiler_params=pltpu.CompilerParams(dimension_semantics=("arbitrary",)),
        name="rope_tables",
    )(pos_dense, invf_dense, _rope_spread_matrix())
    return cos.reshape(bsz, seq, LANES), sin.reshape(bsz, seq, LANES)


def _layer_params(l, norm1_w, w_in, conv_w, conv_b, gla_wa2, gla_ba, gla_onorm_w, q_norm_w, k_norm_w,
                  sinks, w_branch, w_o, norm2_w, w_up, w_down):
    w = w_in[l]
    ga0 = 3 * CONV_DIM + 2 * GLA_HEADS * GLA_DK + 2 * GLA_HEADS * GLA_DV
    ga_cols = jnp.pad(w[:, ga0:ga0 + GLA_RANK], ((0, 0), (0, GA_PAD - GLA_RANK)))
    w_packed = jnp.concatenate([w[:, :ga0], ga_cols, w[:, ga0 + GLA_RANK:]], axis=1).astype(BF16)
    return {
        "sinks": sinks[l].astype(F32),
        "n1w": norm1_w[l][None, :],
        "w_in": w_packed,
        "conv_w": conv_w[l],
        "conv_b": conv_b[l][None, :],
        "wa2": jnp.pad(gla_wa2[l], ((0, GA_PAD - GLA_RANK), (0, 0))).astype(BF16),
        "ba": gla_ba[l][None, :],
        "onw": gla_onorm_w[l][None, :],
        "qnw": jnp.tile(q_norm_w[l], LANES // SWA_HEAD_DIM)[None, :],
        "knw": jnp.tile(k_norm_w[l], LANES // SWA_HEAD_DIM)[None, :],
        "w_branch": w_branch[l].astype(BF16),
        "w_o": w_o[l].astype(BF16),
        "n2w": norm2_w[l][None, :],
        "w_up": w_up[l].astype(BF16),
        "w_down": w_down[l].astype(BF16),
    }


def kernel(x, positions, norm1_w, w_in, conv_w, conv_b, gla_wa2, gla_ba, gla_onorm_w, q_norm_w, k_norm_w,
           sinks, w_branch, w_o, norm2_w, w_up, w_down):
    bsz, seq, d = x.shape
    assert d == D_MODEL and w_in.shape[2] == N_IN_PACKED - GA_PAD + GLA_RANK
    tok = min(TOK_TILE, seq)
    assert seq % tok == 0 and tok % (2 * SWA_BLOCK) == 0
    ffn_tile = min(FFN_TILE, bsz * seq)
    assert (bsz * seq) % ffn_tile == 0
    cos, sin = _rope_tables(positions)
    for l in range(w_in.shape[0]):
        p = _layer_params(l, norm1_w, w_in, conv_w, conv_b, gla_wa2, gla_ba, gla_onorm_w, q_norm_w,
                          k_norm_w, sinks, w_branch, w_o, norm2_w, w_up, w_down)
        x = _mixer_call(x, cos, sin, p, tok)
        x = _ffn_call(x.reshape(bsz * seq, d), p, ffn_tile).reshape(bsz, seq, d)
    return x
```

```python
import functools

import jax
import jax.numpy as jnp
import numpy as np
from jax import lax
from jax.experimental import pallas as pl
from jax.experimental.pallas import tpu as pltpu

F32 = jnp.float32
BF16 = jnp.bfloat16

D_MODEL = 1024
CONV_DIM = 512
CONV_WIDTH = 3
GLA_HEADS = 4
GLA_DK = 64
GLA_DV = 128
GLA_RANK = 16
GLA_TAU = 16.0
GLA_CHUNK = 64
SWA_HEADS = 8
SWA_KV_HEADS = 2
SWA_GROUP = SWA_HEADS // SWA_KV_HEADS
SWA_HEAD_DIM = 64
SWA_BLOCK = 128
ROPE_THETA = 500000.0
ROPE_DIM = SWA_HEAD_DIM // 4
N_BRANCHES = 3
BRANCH_WIDTH = 512
D_FF = 4 * D_MODEL
NORM_EPS = 1e-6
LOG2_E = 1.4426950408889634

LANES = 128
SUBLANES = 8
V7X_VMEM_BYTES = 64 * 1024 * 1024

GA_PAD = LANES
IN_A = 3 * CONV_DIM + 2 * GLA_HEADS * GLA_DK + 2 * GLA_HEADS * GLA_DV
IN_B = SWA_HEADS * SWA_HEAD_DIM + 2 * SWA_KV_HEADS * SWA_HEAD_DIM + N_BRANCHES * D_MODEL
N_IN = IN_A + GLA_RANK + IN_B
A_CONV = 0
A_GQ = A_CONV + 3 * CONV_DIM
A_GK = A_GQ + GLA_HEADS * GLA_DK
A_GV = A_GK + GLA_HEADS * GLA_DK
A_GR = A_GV + GLA_HEADS * GLA_DV
B_SQ = 0
B_SK = B_SQ + SWA_HEADS * SWA_HEAD_DIM
B_SV = B_SK + SWA_KV_HEADS * SWA_HEAD_DIM
B_GATE = B_SV + SWA_KV_HEADS * SWA_HEAD_DIM

ROPE_TOKENS_PER_ROW = LANES // (ROPE_DIM // 2)
ROPE_TILE = 1024
SUB_TILE = 256
TOK_TILE = 512
FFN_TILE = 1024
FF_CHUNK = 1024


def _dot(a, b):
    return jnp.dot(a, b, preferred_element_type=F32)


def _dot_nt(a, b):
    return lax.dot_general(a, b, (((1,), (1,)), ((), ())), preferred_element_type=F32)


def _split_dot(m_bf16, v_f32):
    hi = v_f32.astype(BF16)
    lo = (v_f32 - hi.astype(F32)).astype(BF16)
    return _dot(m_bf16, hi) + _dot(m_bf16, lo)


def _split_dot_rhs(v_f32, m_bf16):
    hi = v_f32.astype(BF16)
    lo = (v_f32 - hi.astype(F32)).astype(BF16)
    return _dot(hi, m_bf16) + _dot(lo, m_bf16)


def _rms_scale(x):
    return x * lax.rsqrt(jnp.mean(x * x, axis=-1, keepdims=True) + NORM_EPS)


def _lane_group_mask(width, group, index):
    lane = lax.broadcasted_iota(jnp.int32, (1, width), 1)
    return jnp.where((lane // group) == index, 1.0, 0.0).astype(F32)


def _split3(v_f32):
    hi = v_f32.astype(BF16)
    r1 = v_f32 - hi.astype(F32)
    mid = r1.astype(BF16)
    lo = (r1 - mid.astype(F32)).astype(BF16)
    return hi, mid, lo


def _rope_table_kernel(pos_ref, invf_ref, spread_ref, cos_ref, sin_ref, *, tokens):
    rows = tokens // ROPE_TOKENS_PER_ROW
    ang = pos_ref[...].astype(F32) * invf_ref[...]
    dense = jnp.concatenate([jnp.cos(ang), jnp.sin(ang)], axis=1)
    by_tok = jnp.broadcast_to(dense[:, None, :], (rows, ROPE_TOKENS_PER_ROW, 2 * LANES)).reshape(tokens, 2 * LANES)
    tok_c = lax.broadcasted_iota(jnp.int32, (tokens, 2 * LANES), 0) % ROPE_TOKENS_PER_ROW
    slot_c = (lax.broadcasted_iota(jnp.int32, (tokens, 2 * LANES), 1) % LANES) // (ROPE_DIM // 2)
    own = jnp.where(tok_c == slot_c, by_tok, 0.0)
    out = sum(_dot(piece, spread_ref[...]) for piece in _split3(own))
    lane_in_head = lax.broadcasted_iota(jnp.int32, (1, LANES), 1) % SWA_HEAD_DIM
    cos_ref[...] = out[:, 0:LANES] + jnp.where(lane_in_head >= ROPE_DIM, 1.0, 0.0)
    sin_ref[...] = out[:, LANES:2 * LANES]


def _rope(x, cos, sin_signed, lower_half):
    half = ROPE_DIM // 2
    up = pltpu.roll(x, LANES - half, axis=1)
    down = pltpu.roll(x, half, axis=1)
    return x * cos + jnp.where(lower_half, up, down) * sin_signed


def _mixer_kernel(sinks_ref, x_ref, cos_ref, sin_ref, n1w_ref, wina_ref, winga_ref, winb_ref, convw_ref, convb_ref,
                  wa2_ref, ba_ref, onw_ref, qnw_ref, knw_ref, wbr_ref, wo_ref, out_ref,
                  conv_buf, gla_state, k_prev, v_prev, *, tok, layer):
    t = pl.program_id(1)

    @pl.when(t == 0)
    def _():
        conv_buf[0:SUBLANES, :] = jnp.zeros((SUBLANES, CONV_DIM), F32)
        gla_state[...] = jnp.zeros_like(gla_state)
        k_prev[...] = jnp.zeros_like(k_prev)
        v_prev[...] = jnp.zeros_like(v_prev)

    weights = (n1w_ref, wina_ref, winga_ref, winb_ref, convw_ref, convb_ref, wa2_ref, ba_ref, onw_ref, qnw_ref,
               knw_ref, wbr_ref, wo_ref)
    state = (conv_buf, gla_state, k_prev, v_prev)
    sinks = [sinks_ref[layer, h] * LOG2_E for h in range(SWA_HEADS)]
    for s in range(tok // SUB_TILE):
        rows = slice(s * SUB_TILE, (s + 1) * SUB_TILE)
        no_prev_bias = jnp.where(t == 0, -jnp.inf, 0.0) if s == 0 else None
        out_ref[0, rows, :] = _mixer_subtile(x_ref[0, rows, :], cos_ref[0, rows, :], sin_ref[0, rows, :],
                                             no_prev_bias, sinks, weights, state)


def _mixer_subtile(x, cos, sin_signed, no_prev_bias, sinks, weights, state):
    (n1w_ref, wina_ref, winga_ref, winb_ref, convw_ref, convb_ref, wa2_ref, ba_ref, onw_ref, qnw_ref,
     knw_ref, wbr_ref, wo_ref) = weights
    conv_buf, gla_state, k_prev, v_prev = state
    tok = SUB_TILE
    n_chunks = tok // GLA_CHUNK
    n_blocks = tok // SWA_BLOCK

    ub = (_rms_scale(x) * n1w_ref[...]).astype(BF16)

    gate_cols = 2 * LANES
    gate_pieces = []

    def gate_slices(count):
        for _ in range(count):
            c0 = B_GATE + len(gate_pieces) * gate_cols
            gate_pieces.append(jax.nn.sigmoid(_dot(ub, winb_ref[:, c0:c0 + gate_cols])))

    ga = _dot(ub, winga_ref[...])
    pg = _dot(ub, wina_ref[:, A_GQ:IN_A])
    gq = pg[:, 0:A_GK - A_GQ] * (GLA_DK ** -0.5)
    gk = pg[:, A_GK - A_GQ:A_GV - A_GQ]
    gv = pg[:, A_GV - A_GQ:A_GR - A_GQ]
    gr = pg[:, A_GR - A_GQ:IN_A - A_GQ]
    ps = _dot(ub, winb_ref[:, B_SQ:B_GATE])
    sq = ps[:, B_SQ:B_SK]
    sk = ps[:, B_SK:B_SV]
    sv = ps[:, B_SV:B_GATE]

    z = _dot(ga.astype(BF16), wa2_ref[...]) + ba_ref[...]
    glog = (jnp.minimum(z, 0.0) - jnp.log(1.0 + jnp.exp(-jnp.abs(z)))) * (1.0 / GLA_TAU)
    gate_slices(1)

    gi = lax.broadcasted_iota(jnp.int32, (2 * LANES, 2 * LANES), 0) // SWA_HEAD_DIM
    gj = lax.broadcasted_iota(jnp.int32, (2 * LANES, 2 * LANES), 1) // SWA_HEAD_DIM
    head_ones = jnp.where(gi == gj, 1.0, 0.0).astype(BF16)
    n_qcols = SWA_HEADS * SWA_HEAD_DIM // (2 * LANES)
    q_halves = [sq[:, j * 2 * LANES:(j + 1) * 2 * LANES] for j in range(n_qcols)]
    ssq = [_dot((qj * qj).astype(BF16), head_ones) for qj in q_halves]
    ssk = _dot((sk * sk).astype(BF16), head_ones[0:LANES, 0:LANES])

    pc = _dot(ub, wina_ref[:, A_CONV:A_GQ])
    cx = pc[:, 0:CONV_DIM]
    cb = pc[:, CONV_DIM:2 * CONV_DIM]
    cc = pc[:, 2 * CONV_DIM:3 * CONV_DIM]
    uc = cc * cx
    conv_buf[SUBLANES:SUBLANES + tok, :] = uc
    u1 = conv_buf[SUBLANES - 1:SUBLANES - 1 + tok, :]
    u2 = conv_buf[SUBLANES - 2:SUBLANES - 2 + tok, :]
    yconv = convw_ref[0:1, :] * u2 + convw_ref[1:2, :] * u1 + convw_ref[2:3, :] * uc
    ya = (cb * (yconv + convb_ref[...])).astype(BF16)
    conv_buf[0:SUBLANES, :] = uc[tok - SUBLANES:tok, :]

    ri = lax.broadcasted_iota(jnp.int32, (tok, tok), 0)
    ci = lax.broadcasted_iota(jnp.int32, (tok, tok), 1)
    same_chunk = (ri // GLA_CHUNK) == (ci // GLA_CHUNK)
    causal_chunk = same_chunk & (ci <= ri)
    l_tri = jnp.where(causal_chunk, 1.0, 0.0).astype(BF16)
    b = _split_dot(l_tri, glog)
    b_last = jnp.concatenate(
        [jnp.broadcast_to(b[(c + 1) * GLA_CHUNK - 1:(c + 1) * GLA_CHUNK, :], (GLA_CHUNK, GLA_HEADS * GLA_DK))
         for c in range(n_chunks)], axis=0)
    gate_slices(2)
    q_in_b = (gq * jnp.exp(b)).astype(BF16)
    k_in = (gk * jnp.exp(-b)).astype(BF16)
    k_end_b = (gk * jnp.exp(b_last - b)).astype(BF16)
    decay = jnp.exp(b_last)
    head_masks = [_lane_group_mask(GLA_HEADS * GLA_DK, GLA_DK, h).astype(BF16) for h in range(GLA_HEADS)]
    gvb = gv.astype(BF16)

    lower_half = (lax.broadcasted_iota(jnp.int32, (1, LANES), 1) % SWA_HEAD_DIM) < ROPE_DIM // 2
    q_cols = []
    for j in range(n_qcols):
        qn = q_halves[j] * lax.rsqrt(ssq[j] * (1.0 / SWA_HEAD_DIM) + NORM_EPS)
        for i in range(2):
            qn_i = qn[:, i * LANES:(i + 1) * LANES] * qnw_ref[...]
            q_cols.append((_rope(qn_i, cos, sin_signed, lower_half) * (SWA_HEAD_DIM ** -0.5 * LOG2_E)).astype(BF16))
    kn = sk * lax.rsqrt(ssk * (1.0 / SWA_HEAD_DIM) + NORM_EPS) * knw_ref[...]
    kr = _rope(kn, cos, sin_signed, lower_half)

    upd = []
    for c in range(n_chunks):
        r0 = c * GLA_CHUNK
        ke_c = k_end_b[r0:r0 + GLA_CHUNK, :]
        k_stack = jnp.concatenate([ke_c * head_masks[h] for h in range(GLA_HEADS)], axis=0)
        v_stack = jnp.concatenate(
            [gv[r0:r0 + GLA_CHUNK, h * GLA_DV:(h + 1) * GLA_DV] for h in range(GLA_HEADS)], axis=0)
        upd.append(_dot(v_stack.T.astype(BF16), k_stack))

    group_masks = [_lane_group_mask(2 * LANES, SWA_HEAD_DIM, g).astype(BF16) for g in range(SWA_GROUP)]
    lane128 = lax.broadcasted_iota(jnp.int32, (1, LANES), 1)
    slot_j = lax.broadcasted_iota(jnp.int32, (SWA_BLOCK, SWA_GROUP * SWA_BLOCK), 0)
    qry_i = lax.broadcasted_iota(jnp.int32, (SWA_BLOCK, SWA_GROUP * SWA_BLOCK), 1) % SWA_BLOCK
    from_prev = slot_j > qry_i

    def kv_keys(k_blk):
        k_sw = pltpu.roll(k_blk, SWA_HEAD_DIM, axis=1)
        reps = []
        for kv in range(SWA_KV_HEADS):
            k_one = jnp.where((lane128 // SWA_HEAD_DIM) == kv, k_blk, k_sw).astype(BF16)
            reps.append(jnp.concatenate([k_one, k_one], axis=1))
        return reps

    keys = [kv_keys(k_prev[...])] + [kv_keys(kr[n * SWA_BLOCK:(n + 1) * SWA_BLOCK, :]) for n in range(n_blocks)]
    vt_f32 = [sv[n * SWA_BLOCK:(n + 1) * SWA_BLOCK, :].T for n in range(n_blocks)]
    vt = [v_prev[...].astype(BF16)] + [v.astype(BF16) for v in vt_f32]

    def swa_scores(n, kv):
        r0 = n * SWA_BLOCK
        q_kv = jnp.concatenate(q_cols[2 * kv:2 * kv + 2], axis=1)[r0:r0 + SWA_BLOCK, :]
        q_stack = jnp.concatenate([q_kv * group_masks[g] for g in range(SWA_GROUP)], axis=0)
        k_both = jnp.concatenate([keys[n][kv], keys[n + 1][kv]], axis=0)
        return _dot_nt(k_both, q_stack)

    def swa_softmax(n, kv, s_both):
        s_prev = s_both[0:SWA_BLOCK, :]
        if n == 0 and no_prev_bias is not None:
            s_prev = s_prev + no_prev_bias
        s_t = jnp.where(from_prev, s_prev, s_both[SWA_BLOCK:2 * SWA_BLOCK, :])
        sink = jnp.concatenate(
            [jnp.full((1, SWA_BLOCK), sinks[kv * SWA_GROUP + g], F32) for g in range(SWA_GROUP)], axis=1)
        m = jnp.maximum(jnp.max(s_t, axis=0, keepdims=True), sink)
        p_t = jnp.exp2(s_t - m)
        inv = 1.0 / (jnp.sum(p_t, axis=0, keepdims=True) + jnp.exp2(sink - m))
        p_both = jnp.concatenate([jnp.where(from_prev, p_t, 0.0), jnp.where(from_prev, 0.0, p_t)], axis=0)
        return p_both.astype(BF16), inv

    def swa_values(n, kv, p_both, inv):
        hd = slice(kv * SWA_HEAD_DIM, (kv + 1) * SWA_HEAD_DIM)
        vt_both = jnp.concatenate([vt[n][hd, :], vt[n + 1][hd, :]], axis=1)
        o_t = _dot(vt_both, p_both) * inv
        return [o_t[:, g * SWA_BLOCK:(g + 1) * SWA_BLOCK] for g in range(SWA_GROUP)]

    def gla_scores(h):
        att = _dot_nt(q_in_b * head_masks[h], k_in)
        return jnp.where(causal_chunk, att, 0.0).astype(BF16)

    def gla_values(h, att):
        return _dot(att, gvb[:, h * GLA_DV:(h + 1) * GLA_DV])

    steps = [(n, kv) for n in range(n_blocks) for kv in range(SWA_KV_HEADS)]
    assert len(steps) == 4 and GLA_HEADS == 4 and n_chunks == 4

    s0 = swa_scores(*steps[0])
    att0 = gla_scores(0)
    att1 = gla_scores(1)

    state = gla_state[...]
    o_inter = [[] for _ in range(GLA_HEADS)]
    for c in range(n_chunks):
        r0 = c * GLA_CHUNK
        q_c = q_in_b[r0:r0 + GLA_CHUNK, :]
        q_stack = jnp.concatenate([q_c * head_masks[h] for h in range(GLA_HEADS)], axis=0)
        oi = _dot_nt(q_stack, state.astype(BF16))
        for h in range(GLA_HEADS):
            o_inter[h].append(oi[h * GLA_CHUNK:(h + 1) * GLA_CHUNK, :])
        state = decay[r0:r0 + 1, :] * state + upd[c]
    gla_state[...] = state

    s1 = swa_scores(*steps[1])
    o_heads = swa_values(*steps[0], *swa_softmax(*steps[0], s0))
    ov0 = gla_values(0, att0)
    ov1 = gla_values(1, att1)
    att2 = gla_scores(2)
    att3 = gla_scores(3)
    yc_blocks = []
    s2 = swa_scores(*steps[2])
    o_heads += swa_values(*steps[1], *swa_softmax(*steps[1], s1))
    yc_blocks.append(jnp.concatenate(o_heads, axis=0).T)
    gate_slices(1)
    ov2 = gla_values(2, att2)
    ov3 = gla_values(3, att3)
    s3 = swa_scores(*steps[3])
    o_heads = swa_values(*steps[2], *swa_softmax(*steps[2], s2))
    gate_slices(2)
    o_heads += swa_values(*steps[3], *swa_softmax(*steps[3], s3))
    yc_blocks.append(jnp.concatenate(o_heads, axis=0).T)
    yc = jnp.concatenate(yc_blocks, axis=0).astype(BF16)
    k_prev[...] = kr[tok - SWA_BLOCK:tok, :]
    v_prev[...] = vt_f32[-1]

    br_a = _dot(ya, wbr_ref[0])
    gate_slices(5)

    yb_parts = []
    for h, ov in enumerate((ov0, ov1, ov2, ov3)):
        o_h = ov + jnp.concatenate(o_inter[h], axis=0)
        o_h = _rms_scale(o_h) * onw_ref[...]
        r_h = gr[:, h * GLA_DV:(h + 1) * GLA_DV]
        yb_parts.append((o_h * (r_h * jax.nn.sigmoid(r_h))).astype(BF16))
    yb = jnp.concatenate(yb_parts, axis=1)
    br_b = _dot(yb, wbr_ref[1])
    gate_slices(1)

    per_branch = D_MODEL // gate_cols
    assert len(gate_pieces) == N_BRANCHES * per_branch
    acc = x
    mixed_slices = []
    for j in range(per_branch):
        cols = slice(j * gate_cols, (j + 1) * gate_cols)
        br_c_j = _dot(yc, wbr_ref[2, :, cols])
        mixed_j = (gate_pieces[j] * br_a[:, cols] + gate_pieces[per_branch + j] * br_b[:, cols]
                   + gate_pieces[2 * per_branch + j] * br_c_j)
        mixed_slices.append(mixed_j.astype(BF16))
        if j >= 1:
            acc = acc + _dot(mixed_slices[j - 1], wo_ref[(j - 1) * gate_cols:j * gate_cols, :])
    return acc + _dot(mixed_slices[-1], wo_ref[(per_branch - 1) * gate_cols:per_branch * gate_cols, :])


def _ffn_kernel(x_ref, n2w_ref, wup_ref, wdown_ref, out_ref):
    x = x_ref[...]
    hb = (_rms_scale(x) * n2w_ref[...]).astype(BF16)
    acc = x
    for c in range(D_FF // FF_CHUNK):
        a = jnp.maximum(_dot(hb, wup_ref[:, c * FF_CHUNK:(c + 1) * FF_CHUNK]), 0.0)
        acc = acc + _dot((a * a).astype(BF16), wdown_ref[c * FF_CHUNK:(c + 1) * FF_CHUNK, :])
    out_ref[...] = acc


def _const_spec(shape):
    zeros = (0,) * len(shape)
    return pl.BlockSpec(shape, lambda *_: zeros, pipeline_mode=pl.Buffered(1))


def _layer_spec(layer, shape):
    zeros = (0,) * len(shape)
    return pl.BlockSpec((None,) + tuple(shape), lambda *_: (layer,) + zeros, pipeline_mode=pl.Buffered(1))


def _mixer_call(x, cos, sin, p, layer, tok):
    bsz, seq, d = x.shape
    grid = (bsz, seq // tok)
    tile = lambda b, t: (b, t, 0)
    spec = functools.partial(_layer_spec, layer)
    in_specs = [
        pl.BlockSpec(memory_space=pltpu.SMEM),
        pl.BlockSpec((1, tok, d), tile),
        pl.BlockSpec((1, tok, LANES), tile),
        pl.BlockSpec((1, tok, LANES), tile),
        spec((1, d)),
        spec((d, IN_A)),
        spec((d, GA_PAD)),
        spec((d, IN_B)),
        spec((CONV_WIDTH, CONV_DIM)),
        spec((1, CONV_DIM)),
        spec((GA_PAD, GLA_HEADS * GLA_DK)),
        spec((1, GLA_HEADS * GLA_DK)),
        spec((1, GLA_DV)),
        spec((1, LANES)),
        spec((1, LANES)),
        spec((N_BRANCHES, BRANCH_WIDTH, d)),
        spec((d, d)),
    ]
    scratch = [
        pltpu.VMEM((SUB_TILE + SUBLANES, CONV_DIM), F32),
        pltpu.VMEM((GLA_DV, GLA_HEADS * GLA_DK), F32),
        pltpu.VMEM((SWA_BLOCK, SWA_KV_HEADS * SWA_HEAD_DIM), F32),
        pltpu.VMEM((SWA_BLOCK, SWA_KV_HEADS * SWA_HEAD_DIM), F32),
    ]
    return pl.pallas_call(
        functools.partial(_mixer_kernel, tok=tok, layer=layer),
        grid=grid,
        in_specs=in_specs,
        out_specs=pl.BlockSpec((1, tok, d), tile),
        out_shape=jax.ShapeDtypeStruct(x.shape, F32),
        scratch_shapes=scratch,
        compiler_params=pltpu.CompilerParams(
            dimension_semantics=("arbitrary", "arbitrary"),
            vmem_limit_bytes=V7X_VMEM_BYTES * 3 // 4),
        name="mixer",
    )(p["sinks"], x, cos, sin, p["n1w"], p["w_in_a"], p["w_in_ga"], p["w_in_b"], p["conv_w"], p["conv_b"],
      p["wa2"], p["ba"], p["onw"], p["qnw"], p["knw"], p["w_branch"], p["w_o"])


def _ffn_call(x2, p, layer, tile):
    n, d = x2.shape
    spec = functools.partial(_layer_spec, layer)
    return pl.pallas_call(
        _ffn_kernel,
        grid=(n // tile,),
        in_specs=[
            pl.BlockSpec((tile, d), lambda i: (i, 0)),
            spec((1, d)),
            spec((d, D_FF)),
            spec((D_FF, d)),
        ],
        out_specs=pl.BlockSpec((tile, d), lambda i: (i, 0)),
        out_shape=jax.ShapeDtypeStruct(x2.shape, F32),
        compiler_params=pltpu.CompilerParams(
            dimension_semantics=("arbitrary",),
            vmem_limit_bytes=V7X_VMEM_BYTES * 3 // 4),
        name="ffn",
    )(x2, p["n2w"], p["w_up"], p["w_down"])


def _rope_spread_matrix():
    half = ROPE_DIM // 2
    c = np.arange(LANES)[:, None]
    l = np.arange(LANES)[None, :]
    hit = ((c % half) == (l % half)) & ((l % SWA_HEAD_DIM) < ROPE_DIM)
    sign = np.where((l % SWA_HEAD_DIM) < half, -1.0, 1.0)
    m = np.zeros((2 * LANES, 2 * LANES), np.float32)
    m[:LANES, :LANES] = hit
    m[LANES:, LANES:] = hit * sign
    return jnp.asarray(m, BF16)


def _rope_tables(positions):
    bsz, seq = positions.shape
    n = bsz * seq
    tokens = min(ROPE_TILE, n)
    assert n % tokens == 0 and tokens % (ROPE_TOKENS_PER_ROW * SUBLANES) == 0
    half = ROPE_DIM // 2
    inv_freq = ROPE_THETA ** (-jnp.arange(0, ROPE_DIM, 2, dtype=F32) / ROPE_DIM)
    invf_dense = jnp.tile(inv_freq, LANES // half)[None, :]
    pos_dense = jnp.repeat(positions.astype(jnp.int32).reshape(n), half).reshape(n // ROPE_TOKENS_PER_ROW, LANES)
    rows = tokens // ROPE_TOKENS_PER_ROW
    cos, sin = pl.pallas_call(
        functools.partial(_rope_table_kernel, tokens=tokens),
        grid=(n // tokens,),
        in_specs=[
            pl.BlockSpec((rows, LANES), lambda i: (i, 0)),
            _const_spec((1, LANES)),
            _const_spec((2 * LANES, 2 * LANES)),
        ],
        out_specs=[pl.BlockSpec((tokens, LANES), lambda i: (i, 0))] * 2,
        out_shape=[jax.ShapeDtypeStruct((n, LANES), F32)] * 2,
        compiler_params=pltpu.CompilerParams(dimension_semantics=("arbitrary",)),
        name="rope_tables",
    )(pos_dense, invf_dense, _rope_spread_matrix())
    return cos.reshape(bsz, seq, LANES), sin.reshape(bsz, seq, LANES)


def _prepare_params(norm1_w, w_in, conv_w, conv_b, gla_wa2, gla_ba, gla_onorm_w, q_norm_w, k_norm_w,
                    sinks, w_branch, w_o, norm2_w, w_up, w_down):
    heads_per_tile = LANES // SWA_HEAD_DIM
    w_in_b16 = w_in.astype(BF16)
    return {
        "sinks": sinks.astype(F32),
        "n1w": norm1_w[:, None, :],
        "w_in_a": w_in_b16[:, :, :IN_A],
        "w_in_ga": jnp.pad(w_in_b16[:, :, IN_A:IN_A + GLA_RANK], ((0, 0), (0, 0), (0, GA_PAD - GLA_RANK))),
        "w_in_b": w_in_b16[:, :, IN_A + GLA_RANK:],
        "conv_w": conv_w,
        "conv_b": conv_b[:, None, :],
        "wa2": jnp.pad(gla_wa2, ((0, 0), (0, GA_PAD - GLA_RANK), (0, 0))).astype(BF16),
        "ba": gla_ba[:, None, :],
        "onw": gla_onorm_w[:, None, :],
        "qnw": jnp.tile(q_norm_w, (1, heads_per_tile))[:, None, :],
        "knw": jnp.tile(k_norm_w, (1, heads_per_tile))[:, None, :],
        "w_branch": w_branch.astype(BF16),
        "w_o": w_o.astype(BF16),
        "n2w": norm2_w[:, None, :],
        "w_up": w_up.astype(BF16),
        "w_down": w_down.astype(BF16),
    }


def kernel(x, positions, norm1_w, w_in, conv_w, conv_b, gla_wa2, gla_ba, gla_onorm_w, q_norm_w, k_norm_w,
           sinks, w_branch, w_o, norm2_w, w_up, w_down):
    bsz, seq, d = x.shape
    assert d == D_MODEL and w_in.shape[2] == N_IN
    tok = min(TOK_TILE, seq)
    assert seq % tok == 0 and tok % SUB_TILE == 0 and SUB_TILE % (2 * SWA_BLOCK) == 0
    ffn_tile = min(FFN_TILE, bsz * seq)
    assert (bsz * seq) % ffn_tile == 0
    cos, sin = _rope_tables(positions)
    p = _prepare_params(norm1_w, w_in, conv_w, conv_b, gla_wa2, gla_ba, gla_onorm_w, q_norm_w, k_norm_w,
                        sinks, w_branch, w_o, norm2_w, w_up, w_down)
    for layer in range(w_in.shape[0]):
        x = _mixer_call(x, cos, sin, p, layer, tok)
        x = _ffn_call(x.reshape(bsz * seq, d), p, layer, ffn_tile).reshape(bsz, seq, d)
    return x
```

```python
import functools

import jax
import jax.numpy as jnp
import numpy as np
from jax import lax
from jax.experimental import pallas as pl
from jax.experimental.pallas import tpu as pltpu

F32 = jnp.float32
BF16 = jnp.bfloat16

D_MODEL = 1024
CONV_DIM = 512
CONV_WIDTH = 3
GLA_HEADS = 4
GLA_DK = 64
GLA_DV = 128
GLA_RANK = 16
GLA_TAU = 16.0
GLA_CHUNK = 64
SWA_HEADS = 8
SWA_KV_HEADS = 2
SWA_GROUP = SWA_HEADS // SWA_KV_HEADS
SWA_HEAD_DIM = 64
SWA_BLOCK = 128
ROPE_THETA = 500000.0
ROPE_DIM = SWA_HEAD_DIM // 4
N_BRANCHES = 3
BRANCH_WIDTH = 512
D_FF = 4 * D_MODEL
NORM_EPS = 1e-6
LOG2_E = 1.4426950408889634

LANES = 128
SUBLANES = 8
V7X_VMEM_BYTES = 64 * 1024 * 1024

GA_PAD = LANES
IN_A = 3 * CONV_DIM + 2 * GLA_HEADS * GLA_DK + 2 * GLA_HEADS * GLA_DV
IN_B = SWA_HEADS * SWA_HEAD_DIM + 2 * SWA_KV_HEADS * SWA_HEAD_DIM + N_BRANCHES * D_MODEL
N_IN = IN_A + GLA_RANK + IN_B
A_CONV = 0
A_GQ = A_CONV + 3 * CONV_DIM
A_GK = A_GQ + GLA_HEADS * GLA_DK
A_GV = A_GK + GLA_HEADS * GLA_DK
A_GR = A_GV + GLA_HEADS * GLA_DV
B_SQ = 0
B_SK = B_SQ + SWA_HEADS * SWA_HEAD_DIM
B_SV = B_SK + SWA_KV_HEADS * SWA_HEAD_DIM
B_GATE = B_SV + SWA_KV_HEADS * SWA_HEAD_DIM

ROPE_TOKENS_PER_ROW = LANES // (ROPE_DIM // 2)
ROPE_TILE = 1024
SUB_TILE = 256
TOK_TILE = 512
FFN_TILE = 1024
FF_CHUNK = 1024


def _dot(a, b):
    return jnp.dot(a, b, preferred_element_type=F32)


def _dot_nt(a, b):
    return lax.dot_general(a, b, (((1,), (1,)), ((), ())), preferred_element_type=F32)


def _split_dot(m_bf16, v_f32):
    hi = v_f32.astype(BF16)
    lo = (v_f32 - hi.astype(F32)).astype(BF16)
    return _dot(m_bf16, hi) + _dot(m_bf16, lo)


def _split_dot_rhs(v_f32, m_bf16):
    hi = v_f32.astype(BF16)
    lo = (v_f32 - hi.astype(F32)).astype(BF16)
    return _dot(hi, m_bf16) + _dot(lo, m_bf16)


def _rms_scale(x):
    return x * lax.rsqrt(jnp.mean(x * x, axis=-1, keepdims=True) + NORM_EPS)


def _lane_group_mask(width, group, index):
    lane = lax.broadcasted_iota(jnp.int32, (1, width), 1)
    return jnp.where((lane // group) == index, 1.0, 0.0).astype(F32)


def _split3(v_f32):
    hi = v_f32.astype(BF16)
    r1 = v_f32 - hi.astype(F32)
    mid = r1.astype(BF16)
    lo = (r1 - mid.astype(F32)).astype(BF16)
    return hi, mid, lo


def _rope_table_kernel(pos_ref, invf_ref, spread_ref, cos_ref, sin_ref, *, tokens):
    rows = tokens // ROPE_TOKENS_PER_ROW
    ang = pos_ref[...].astype(F32) * invf_ref[...]
    dense = jnp.concatenate([jnp.cos(ang), jnp.sin(ang)], axis=1)
    by_tok = jnp.broadcast_to(dense[:, None, :], (rows, ROPE_TOKENS_PER_ROW, 2 * LANES)).reshape(tokens, 2 * LANES)
    tok_c = lax.broadcasted_iota(jnp.int32, (tokens, 2 * LANES), 0) % ROPE_TOKENS_PER_ROW
    slot_c = (lax.broadcasted_iota(jnp.int32, (tokens, 2 * LANES), 1) % LANES) // (ROPE_DIM // 2)
    own = jnp.where(tok_c == slot_c, by_tok, 0.0)
    out = sum(_dot(piece, spread_ref[...]) for piece in _split3(own))
    lane_in_head = lax.broadcasted_iota(jnp.int32, (1, LANES), 1) % SWA_HEAD_DIM
    cos_ref[...] = out[:, 0:LANES] + jnp.where(lane_in_head >= ROPE_DIM, 1.0, 0.0)
    sin_ref[...] = out[:, LANES:2 * LANES]


def _rope(x, cos, sin_signed, lower_half):
    half = ROPE_DIM // 2
    up = pltpu.roll(x, LANES - half, axis=1)
    down = pltpu.roll(x, half, axis=1)
    return x * cos + jnp.where(lower_half, up, down) * sin_signed


def _mixer_kernel(sinks_ref, x_ref, cos_ref, sin_ref, n1w_ref, wina_ref, winga_ref, winb_ref, convw_ref, convb_ref,
                  wa2_ref, ba_ref, onw_ref, qnw_ref, knw_ref, wbr_ref, wo_ref, out_ref,
                  conv_buf, gla_state, k_prev, v_prev, *, tok, layer):
    t = pl.program_id(1)

    @pl.when(t == 0)
    def _():
        conv_buf[0:SUBLANES, :] = jnp.zeros((SUBLANES, CONV_DIM), F32)
        gla_state[...] = jnp.zeros_like(gla_state)
        k_prev[...] = jnp.zeros_like(k_prev)
        v_prev[...] = jnp.zeros_like(v_prev)

    weights = (n1w_ref, wina_ref, winga_ref, winb_ref, convw_ref, convb_ref, wa2_ref, ba_ref, onw_ref, qnw_ref,
               knw_ref, wbr_ref, wo_ref)
    state = (conv_buf, gla_state, k_prev, v_prev)
    sinks = [sinks_ref[layer, h] * LOG2_E for h in range(SWA_HEADS)]
    for s in range(tok // SUB_TILE):
        rows = slice(s * SUB_TILE, (s + 1) * SUB_TILE)
        no_prev_bias = jnp.where(t == 0, -jnp.inf, 0.0) if s == 0 else None
        out_ref[0, rows, :] = _mixer_subtile(x_ref[0, rows, :], cos_ref[0, rows, :], sin_ref[0, rows, :],
                                             no_prev_bias, sinks, weights, state)


def _mixer_subtile(x, cos, sin_signed, no_prev_bias, sinks, weights, state):
    (n1w_ref, wina_ref, winga_ref, winb_ref, convw_ref, convb_ref, wa2_ref, ba_ref, onw_ref, qnw_ref,
     knw_ref, wbr_ref, wo_ref) = weights
    conv_buf, gla_state, k_prev, v_prev = state
    tok = SUB_TILE
    n_chunks = tok // GLA_CHUNK
    n_blocks = tok // SWA_BLOCK

    ub = (_rms_scale(x) * n1w_ref[...]).astype(BF16)

    gate_cols = 2 * LANES
    gate_pieces = []

    def gate_slices(count):
        for _ in range(count):
            c0 = B_GATE + len(gate_pieces) * gate_cols
            gate_pieces.append(jax.nn.sigmoid(_dot(ub, winb_ref[:, c0:c0 + gate_cols])))

    ga = _dot(ub, winga_ref[...])
    pg = _dot(ub, wina_ref[:, A_GQ:IN_A])
    gq = pg[:, 0:A_GK - A_GQ] * (GLA_DK ** -0.5)
    gk = pg[:, A_GK - A_GQ:A_GV - A_GQ]
    gv = pg[:, A_GV - A_GQ:A_GR - A_GQ]
    gr = pg[:, A_GR - A_GQ:IN_A - A_GQ]
    ps = _dot(ub, winb_ref[:, B_SQ:B_GATE])
    sq = ps[:, B_SQ:B_SK]
    sk = ps[:, B_SK:B_SV]
    sv = ps[:, B_SV:B_GATE]

    z = _dot(ga.astype(BF16), wa2_ref[...]) + ba_ref[...]
    glog = (jnp.minimum(z, 0.0) - jnp.log(1.0 + jnp.exp(-jnp.abs(z)))) * (1.0 / GLA_TAU)
    gate_slices(1)

    gi = lax.broadcasted_iota(jnp.int32, (2 * LANES, 2 * LANES), 0) // SWA_HEAD_DIM
    gj = lax.broadcasted_iota(jnp.int32, (2 * LANES, 2 * LANES), 1) // SWA_HEAD_DIM
    head_ones = jnp.where(gi == gj, 1.0, 0.0).astype(BF16)
    n_qcols = SWA_HEADS * SWA_HEAD_DIM // (2 * LANES)
    q_halves = [sq[:, j * 2 * LANES:(j + 1) * 2 * LANES] for j in range(n_qcols)]
    ssq = [_dot((qj * qj).astype(BF16), head_ones) for qj in q_halves]
    ssk = _dot((sk * sk).astype(BF16), head_ones[0:LANES, 0:LANES])

    cx = _dot(ub, wina_ref[:, A_CONV:A_CONV + CONV_DIM])
    cc = _dot(ub, wina_ref[:, A_CONV + 2 * CONV_DIM:A_CONV + 3 * CONV_DIM])
    gate_slices(2)
    cb = _dot(ub, wina_ref[:, A_CONV + CONV_DIM:A_CONV + 2 * CONV_DIM])
    uc = cc * cx
    conv_buf[SUBLANES:SUBLANES + tok, :] = uc
    u1 = conv_buf[SUBLANES - 1:SUBLANES - 1 + tok, :]
    u2 = conv_buf[SUBLANES - 2:SUBLANES - 2 + tok, :]
    yconv = convw_ref[0:1, :] * u2 + convw_ref[1:2, :] * u1 + convw_ref[2:3, :] * uc
    ya = (cb * (yconv + convb_ref[...])).astype(BF16)
    conv_buf[0:SUBLANES, :] = uc[tok - SUBLANES:tok, :]

    ri = lax.broadcasted_iota(jnp.int32, (tok, tok), 0)
    ci = lax.broadcasted_iota(jnp.int32, (tok, tok), 1)
    same_chunk = (ri // GLA_CHUNK) == (ci // GLA_CHUNK)
    causal_chunk = same_chunk & (ci <= ri)
    l_tri = jnp.where(causal_chunk, 1.0, 0.0).astype(BF16)
    b = _split_dot(l_tri, glog)
    b_last = jnp.concatenate(
        [jnp.broadcast_to(b[(c + 1) * GLA_CHUNK - 1:(c + 1) * GLA_CHUNK, :], (GLA_CHUNK, GLA_HEADS * GLA_DK))
         for c in range(n_chunks)], axis=0)
    gate_slices(2)
    q_in_b = (gq * jnp.exp(b)).astype(BF16)
    k_in = (gk * jnp.exp(-b)).astype(BF16)
    k_end_b = (gk * jnp.exp(b_last - b)).astype(BF16)
    decay = jnp.exp(b_last)
    head_masks = [_lane_group_mask(GLA_HEADS * GLA_DK, GLA_DK, h).astype(BF16) for h in range(GLA_HEADS)]
    gvb = gv.astype(BF16)

    lower_half = (lax.broadcasted_iota(jnp.int32, (1, LANES), 1) % SWA_HEAD_DIM) < ROPE_DIM // 2
    q_cols = []
    for j in range(n_qcols):
        qn = q_halves[j] * lax.rsqrt(ssq[j] * (1.0 / SWA_HEAD_DIM) + NORM_EPS)
        for i in range(2):
            qn_i = qn[:, i * LANES:(i + 1) * LANES] * qnw_ref[...]
            q_cols.append((_rope(qn_i, cos, sin_signed, lower_half) * (SWA_HEAD_DIM ** -0.5 * LOG2_E)).astype(BF16))
    kn = sk * lax.rsqrt(ssk * (1.0 / SWA_HEAD_DIM) + NORM_EPS) * knw_ref[...]
    kr = _rope(kn, cos, sin_signed, lower_half)

    upd = []
    for c in range(n_chunks):
        r0 = c * GLA_CHUNK
        ke_c = k_end_b[r0:r0 + GLA_CHUNK, :]
        k_stack = jnp.concatenate([ke_c * head_masks[h] for h in range(GLA_HEADS)], axis=0)
        v_stack = jnp.concatenate(
            [gv[r0:r0 + GLA_CHUNK, h * GLA_DV:(h + 1) * GLA_DV] for h in range(GLA_HEADS)], axis=0)
        upd.append(_dot(v_stack.T.astype(BF16), k_stack))

    group_masks = [_lane_group_mask(2 * LANES, SWA_HEAD_DIM, g).astype(BF16) for g in range(SWA_GROUP)]
    lane128 = lax.broadcasted_iota(jnp.int32, (1, LANES), 1)
    slot_j = lax.broadcasted_iota(jnp.int32, (SWA_BLOCK, SWA_GROUP * SWA_BLOCK), 0)
    qry_i = lax.broadcasted_iota(jnp.int32, (SWA_BLOCK, SWA_GROUP * SWA_BLOCK), 1) % SWA_BLOCK
    from_prev = slot_j > qry_i

    def kv_keys(k_blk):
        k_sw = pltpu.roll(k_blk, SWA_HEAD_DIM, axis=1)
        reps = []
        for kv in range(SWA_KV_HEADS):
            k_one = jnp.where((lane128 // SWA_HEAD_DIM) == kv, k_blk, k_sw).astype(BF16)
            reps.append(jnp.concatenate([k_one, k_one], axis=1))
        return reps

    keys = [kv_keys(k_prev[...])] + [kv_keys(kr[n * SWA_BLOCK:(n + 1) * SWA_BLOCK, :]) for n in range(n_blocks)]
    vt_f32 = [sv[n * SWA_BLOCK:(n + 1) * SWA_BLOCK, :].T for n in range(n_blocks)]
    vt = [v_prev[...].astype(BF16)] + [v.astype(BF16) for v in vt_f32]

    def swa_scores(n, kv):
        r0 = n * SWA_BLOCK
        q_kv = jnp.concatenate(q_cols[2 * kv:2 * kv + 2], axis=1)[r0:r0 + SWA_BLOCK, :]
        q_stack = jnp.concatenate([q_kv * group_masks[g] for g in range(SWA_GROUP)], axis=0)
        k_both = jnp.concatenate([keys[n][kv], keys[n + 1][kv]], axis=0)
        return _dot_nt(k_both, q_stack)

    def swa_softmax(n, kv, s_both):
        s_prev = s_both[0:SWA_BLOCK, :]
        if n == 0 and no_prev_bias is not None:
            s_prev = s_prev + no_prev_bias
        s_t = jnp.where(from_prev, s_prev, s_both[SWA_BLOCK:2 * SWA_BLOCK, :])
        sink = jnp.concatenate(
            [jnp.full((1, SWA_BLOCK), sinks[kv * SWA_GROUP + g], F32) for g in range(SWA_GROUP)], axis=1)
        m = jnp.maximum(jnp.max(s_t, axis=0, keepdims=True), sink)
        p_t = jnp.exp2(s_t - m)
        inv = 1.0 / (jnp.sum(p_t, axis=0, keepdims=True) + jnp.exp2(sink - m))
        p_both = jnp.concatenate([jnp.where(from_prev, p_t, 0.0), jnp.where(from_prev, 0.0, p_t)], axis=0)
        return p_both.astype(BF16), inv

    def swa_values(n, kv, p_both, inv):
        hd = slice(kv * SWA_HEAD_DIM, (kv + 1) * SWA_HEAD_DIM)
        vt_both = jnp.concatenate([vt[n][hd, :], vt[n + 1][hd, :]], axis=1)
        o_t = _dot(vt_both, p_both) * inv
        return [o_t[:, g * SWA_BLOCK:(g + 1) * SWA_BLOCK] for g in range(SWA_GROUP)]

    def gla_scores(h):
        att = _dot_nt(q_in_b * head_masks[h], k_in)
        return jnp.where(causal_chunk, att, 0.0).astype(BF16)

    def gla_values(h, att):
        return _dot(att, gvb[:, h * GLA_DV:(h + 1) * GLA_DV])

    steps = [(n, kv) for n in range(n_blocks) for kv in range(SWA_KV_HEADS)]
    assert len(steps) == 4 and GLA_HEADS == 4 and n_chunks == 4

    s0 = swa_scores(*steps[0])
    att0 = gla_scores(0)
    att1 = gla_scores(1)

    state = gla_state[...]
    o_inter = [[] for _ in range(GLA_HEADS)]
    for c in range(n_chunks):
        r0 = c * GLA_CHUNK
        q_c = q_in_b[r0:r0 + GLA_CHUNK, :]
        q_stack = jnp.concatenate([q_c * head_masks[h] for h in range(GLA_HEADS)], axis=0)
        oi = _dot_nt(q_stack, state.astype(BF16))
        for h in range(GLA_HEADS):
            o_inter[h].append(oi[h * GLA_CHUNK:(h + 1) * GLA_CHUNK, :])
        state = decay[r0:r0 + 1, :] * state + upd[c]
    gla_state[...] = state

    s1 = swa_scores(*steps[1])
    o_heads = swa_values(*steps[0], *swa_softmax(*steps[0], s0))
    ov0 = gla_values(0, att0)
    ov1 = gla_values(1, att1)
    att2 = gla_scores(2)
    att3 = gla_scores(3)
    yc_blocks = []
    s2 = swa_scores(*steps[2])
    o_heads += swa_values(*steps[1], *swa_softmax(*steps[1], s1))
    yc_blocks.append(jnp.concatenate(o_heads, axis=0).T)
    gate_slices(1)
    ov2 = gla_values(2, att2)
    ov3 = gla_values(3, att3)
    s3 = swa_scores(*steps[3])
    o_heads = swa_values(*steps[2], *swa_softmax(*steps[2], s2))
    gate_slices(2)
    o_heads += swa_values(*steps[3], *swa_softmax(*steps[3], s3))
    yc_blocks.append(jnp.concatenate(o_heads, axis=0).T)
    yc = jnp.concatenate(yc_blocks, axis=0).astype(BF16)
    k_prev[...] = kr[tok - SWA_BLOCK:tok, :]
    v_prev[...] = vt_f32[-1]

    br_a = _dot(ya, wbr_ref[0])
    gate_slices(3)

    yb_parts = []
    for h, ov in enumerate((ov0, ov1, ov2, ov3)):
        o_h = ov + jnp.concatenate(o_inter[h], axis=0)
        o_h = _rms_scale(o_h) * onw_ref[...]
        r_h = gr[:, h * GLA_DV:(h + 1) * GLA_DV]
        yb_parts.append((o_h * (r_h * jax.nn.sigmoid(r_h))).astype(BF16))
    yb = jnp.concatenate(yb_parts, axis=1)
    br_b = _dot(yb, wbr_ref[1])
    gate_slices(1)

    per_branch = D_MODEL // gate_cols
    assert len(gate_pieces) == N_BRANCHES * per_branch
    acc = x
    mixed_slices = []
    for j in range(per_branch):
        cols = slice(j * gate_cols, (j + 1) * gate_cols)
        br_c_j = _dot(yc, wbr_ref[2, :, cols])
        mixed_j = (gate_pieces[j] * br_a[:, cols] + gate_pieces[per_branch + j] * br_b[:, cols]
                   + gate_pieces[2 * per_branch + j] * br_c_j)
        mixed_slices.append(mixed_j.astype(BF16))
        if j >= 1:
            acc = acc + _dot(mixed_slices[j - 1], wo_ref[(j - 1) * gate_cols:j * gate_cols, :])
    return acc + _dot(mixed_slices[-1], wo_ref[(per_branch - 1) * gate_cols:per_branch * gate_cols, :])


def _ffn_kernel(x_ref, n2w_ref, wup_ref, wdown_ref, out_ref):
    x = x_ref[...]
    hb = (_rms_scale(x) * n2w_ref[...]).astype(BF16)
    acc = x
    for c in range(D_FF // FF_CHUNK):
        a = jnp.maximum(_dot(hb, wup_ref[:, c * FF_CHUNK:(c + 1) * FF_CHUNK]), 0.0)
        acc = acc + _dot((a * a).astype(BF16), wdown_ref[c * FF_CHUNK:(c + 1) * FF_CHUNK, :])
    out_ref[...] = acc


def _const_spec(shape):
    zeros = (0,) * len(shape)
    return pl.BlockSpec(shape, lambda *_: zeros, pipeline_mode=pl.Buffered(1))


def _layer_spec(layer, shape):
    zeros = (0,) * len(shape)
    return pl.BlockSpec((None,) + tuple(shape), lambda *_: (layer,) + zeros, pipeline_mode=pl.Buffered(1))


def _mixer_call(x, cos, sin, p, layer, tok):
    bsz, seq, d = x.shape
    grid = (bsz, seq // tok)
    tile = lambda b, t: (b, t, 0)
    spec = functools.partial(_layer_spec, layer)
    in_specs = [
        pl.BlockSpec(memory_space=pltpu.SMEM),
        pl.BlockSpec((1, tok, d), tile),
        pl.BlockSpec((1, tok, LANES), tile),
        pl.BlockSpec((1, tok, LANES), tile),
        spec((1, d)),
        pl.BlockSpec((None, d, IN_A), lambda *_: (layer, 0, 0), pipeline_mode=pl.Buffered(1)),
        spec((d, GA_PAD)),
        spec((d, IN_B)),
        spec((CONV_WIDTH, CONV_DIM)),
        spec((1, CONV_DIM)),
        spec((GA_PAD, GLA_HEADS * GLA_DK)),
        spec((1, GLA_HEADS * GLA_DK)),
        spec((1, GLA_DV)),
        spec((1, LANES)),
        spec((1, LANES)),
        spec((N_BRANCHES, BRANCH_WIDTH, d)),
        spec((d, d)),
    ]
    scratch = [
        pltpu.VMEM((SUB_TILE + SUBLANES, CONV_DIM), F32),
        pltpu.VMEM((GLA_DV, GLA_HEADS * GLA_DK), F32),
        pltpu.VMEM((SWA_BLOCK, SWA_KV_HEADS * SWA_HEAD_DIM), F32),
        pltpu.VMEM((SWA_BLOCK, SWA_KV_HEADS * SWA_HEAD_DIM), F32),
    ]
    return pl.pallas_call(
        functools.partial(_mixer_kernel, tok=tok, layer=layer),
        grid=grid,
        in_specs=in_specs,
        out_specs=pl.BlockSpec((1, tok, d), tile),
        out_shape=jax.ShapeDtypeStruct(x.shape, F32),
        scratch_shapes=scratch,
        compiler_params=pltpu.CompilerParams(
            dimension_semantics=("arbitrary", "arbitrary"),
            vmem_limit_bytes=V7X_VMEM_BYTES * 3 // 4),
        name="mixer",
    )(p["sinks"], x, cos, sin, p["n1w"], p["w_in"], p["w_in_ga"], p["w_in_b"], p["conv_w"], p["conv_b"],
      p["wa2"], p["ba"], p["onw"], p["qnw"], p["knw"], p["w_branch"], p["w_o"])


def _ffn_call(x2, p, layer, tile):
    n, d = x2.shape
    spec = functools.partial(_layer_spec, layer)
    return pl.pallas_call(
        _ffn_kernel,
        grid=(n // tile,),
        in_specs=[
            pl.BlockSpec((tile, d), lambda i: (i, 0)),
            spec((1, d)),
            spec((d, D_FF)),
            spec((D_FF, d)),
        ],
        out_specs=pl.BlockSpec((tile, d), lambda i: (i, 0)),
        out_shape=jax.ShapeDtypeStruct(x2.shape, F32),
        compiler_params=pltpu.CompilerParams(
            dimension_semantics=("arbitrary",),
            vmem_limit_bytes=V7X_VMEM_BYTES * 3 // 4),
        name="ffn",
    )(x2, p["n2w"], p["w_up"], p["w_down"])


def _rope_spread_matrix():
    half = ROPE_DIM // 2
    c = np.arange(LANES)[:, None]
    l = np.arange(LANES)[None, :]
    hit = ((c % half) == (l % half)) & ((l % SWA_HEAD_DIM) < ROPE_DIM)
    sign = np.where((l % SWA_HEAD_DIM) < half, -1.0, 1.0)
    m = np.zeros((2 * LANES, 2 * LANES), np.float32)
    m[:LANES, :LANES] = hit
    m[LANES:, LANES:] = hit * sign
    return jnp.asarray(m, BF16)


def _rope_tables(positions):
    bsz, seq = positions.shape
    n = bsz * seq
    tokens = min(ROPE_TILE, n)
    assert n % tokens == 0 and tokens % (ROPE_TOKENS_PER_ROW * SUBLANES) == 0
    half = ROPE_DIM // 2
    inv_freq = ROPE_THETA ** (-jnp.arange(0, ROPE_DIM, 2, dtype=F32) / ROPE_DIM)
    invf_dense = jnp.tile(inv_freq, LANES // half)[None, :]
    pos_dense = jnp.repeat(positions.astype(jnp.int32).reshape(n), half).reshape(n // ROPE_TOKENS_PER_ROW, LANES)
    rows = tokens // ROPE_TOKENS_PER_ROW
    cos, sin = pl.pallas_call(
        functools.partial(_rope_table_kernel, tokens=tokens),
        grid=(n // tokens,),
        in_specs=[
            pl.BlockSpec((rows, LANES), lambda i: (i, 0)),
            _const_spec((1, LANES)),
            _const_spec((2 * LANES, 2 * LANES)),
        ],
        out_specs=[pl.BlockSpec((tokens, LANES), lambda i: (i, 0))] * 2,
        out_shape=[jax.ShapeDtypeStruct((n, LANES), F32)] * 2,
        compiler_params=pltpu.CompilerParams(dimension_semantics=("arbitrary",)),
        name="rope_tables",
    )(pos_dense, invf_dense, _rope_spread_matrix())
    return cos.reshape(bsz, seq, LANES), sin.reshape(bsz, seq, LANES)


def _prepare_params(norm1_w, w_in, conv_w, conv_b, gla_wa2, gla_ba, gla_onorm_w, q_norm_w, k_norm_w,
                    sinks, w_branch, w_o, norm2_w, w_up, w_down):
    heads_per_tile = LANES // SWA_HEAD_DIM
    w_in_b16 = lax.optimization_barrier(w_in.astype(BF16))
    return {
        "sinks": sinks.astype(F32),
        "n1w": norm1_w[:, None, :],
        "w_in": w_in_b16,
        "w_in_ga": jnp.pad(w_in_b16[:, :, IN_A:IN_A + GLA_RANK], ((0, 0), (0, 0), (0, GA_PAD - GLA_RANK))),
        "w_in_b": w_in_b16[:, :, IN_A + GLA_RANK:],
        "conv_w": conv_w,
        "conv_b": conv_b[:, None, :],
        "wa2": jnp.pad(gla_wa2, ((0, 0), (0, GA_PAD - GLA_RANK), (0, 0))).astype(BF16),
        "ba": gla_ba[:, None, :],
        "onw": gla_onorm_w[:, None, :],
        "qnw": jnp.tile(q_norm_w, (1, heads_per_tile))[:, None, :],
        "knw": jnp.tile(k_norm_w, (1, heads_per_tile))[:, None, :],
        "w_branch": w_branch.astype(BF16),
        "w_o": w_o.astype(BF16),
        "n2w": norm2_w[:, None, :],
        "w_up": w_up.astype(BF16),
        "w_down": w_down.astype(BF16),
    }


def kernel(x, positions, norm1_w, w_in, conv_w, conv_b, gla_wa2, gla_ba, gla_onorm_w, q_norm_w, k_norm_w,
           sinks, w_branch, w_o, norm2_w, w_up, w_down):
    bsz, seq, d = x.shape
    assert d == D_MODEL and w_in.shape[2] == N_IN
    tok = min(TOK_TILE, seq)
    assert seq % tok == 0 and tok % SUB_TILE == 0 and SUB_TILE % (2 * SWA_BLOCK) == 0
    ffn_tile = min(FFN_TILE, bsz * seq)
    assert (bsz * seq) % ffn_tile == 0
    cos, sin = _rope_tables(positions)
    p = _prepare_params(norm1_w, w_in, conv_w, conv_b, gla_wa2, gla_ba, gla_onorm_w, q_norm_w, k_norm_w,
                        sinks, w_branch, w_o, norm2_w, w_up, w_down)
    for layer in range(w_in.shape[0]):
        x = _mixer_call(x, cos, sin, p, layer, tok)
        x = _ffn_call(x.reshape(bsz * seq, d), p, layer, ffn_tile).reshape(bsz, seq, d)
    return x
```

```python
import functools

import jax
import jax.numpy as jnp
import numpy as np
from jax import lax
from jax.experimental import pallas as pl
from jax.experimental.pallas import tpu as pltpu

F32 = jnp.float32
BF16 = jnp.bfloat16

D_MODEL = 1024
CONV_DIM = 512
CONV_WIDTH = 3
GLA_HEADS = 4
GLA_DK = 64
GLA_DV = 128
GLA_RANK = 16
GLA_TAU = 16.0
GLA_CHUNK = 64
SWA_HEADS = 8
SWA_KV_HEADS = 2
SWA_GROUP = SWA_HEADS // SWA_KV_HEADS
SWA_HEAD_DIM = 64
SWA_BLOCK = 128
ROPE_THETA = 500000.0
ROPE_DIM = SWA_HEAD_DIM // 4
N_BRANCHES = 3
BRANCH_WIDTH = 512
D_FF = 4 * D_MODEL
NORM_EPS = 1e-6
LOG2_E = 1.4426950408889634

LANES = 128
SUBLANES = 8
V7X_VMEM_BYTES = 64 * 1024 * 1024

GA_PAD = LANES
IN_A = 3 * CONV_DIM + 2 * GLA_HEADS * GLA_DK + 2 * GLA_HEADS * GLA_DV
IN_B = SWA_HEADS * SWA_HEAD_DIM + 2 * SWA_KV_HEADS * SWA_HEAD_DIM + N_BRANCHES * D_MODEL
N_IN = IN_A + GLA_RANK + IN_B
A_CONV = 0
A_GQ = A_CONV + 3 * CONV_DIM
A_GK = A_GQ + GLA_HEADS * GLA_DK
A_GV = A_GK + GLA_HEADS * GLA_DK
A_GR = A_GV + GLA_HEADS * GLA_DV
B_SQ = 0
B_SK = B_SQ + SWA_HEADS * SWA_HEAD_DIM
B_SV = B_SK + SWA_KV_HEADS * SWA_HEAD_DIM
B_GATE = B_SV + SWA_KV_HEADS * SWA_HEAD_DIM

ROPE_TOKENS_PER_ROW = LANES // (ROPE_DIM // 2)
ROPE_TILE = 1024
SUB_TILE = 256
TOK_TILE = 1024
FFN_TILE = 1024
FF_CHUNK = 1024


def _dot(a, b):
    return jnp.dot(a, b, preferred_element_type=F32)


def _dot_nt(a, b):
    return lax.dot_general(a, b, (((1,), (1,)), ((), ())), preferred_element_type=F32)


def _split_dot(m_bf16, v_f32):
    hi = v_f32.astype(BF16)
    lo = (v_f32 - hi.astype(F32)).astype(BF16)
    return _dot(m_bf16, hi) + _dot(m_bf16, lo)


def _split_dot_rhs(v_f32, m_bf16):
    hi = v_f32.astype(BF16)
    lo = (v_f32 - hi.astype(F32)).astype(BF16)
    return _dot(hi, m_bf16) + _dot(lo, m_bf16)


def _rms_scale(x):
    return x * lax.rsqrt(jnp.mean(x * x, axis=-1, keepdims=True) + NORM_EPS)


def _lane_group_mask(width, group, index):
    lane = lax.broadcasted_iota(jnp.int32, (1, width), 1)
    return jnp.where((lane // group) == index, 1.0, 0.0).astype(F32)


def _split3(v_f32):
    hi = v_f32.astype(BF16)
    r1 = v_f32 - hi.astype(F32)
    mid = r1.astype(BF16)
    lo = (r1 - mid.astype(F32)).astype(BF16)
    return hi, mid, lo


def _rope_table_kernel(pos_ref, invf_ref, spread_ref, cos_ref, sin_ref, *, tokens):
    rows = tokens // ROPE_TOKENS_PER_ROW
    ang = pos_ref[...].astype(F32) * invf_ref[...]
    dense = jnp.concatenate([jnp.cos(ang), jnp.sin(ang)], axis=1)
    by_tok = jnp.broadcast_to(dense[:, None, :], (rows, ROPE_TOKENS_PER_ROW, 2 * LANES)).reshape(tokens, 2 * LANES)
    tok_c = lax.broadcasted_iota(jnp.int32, (tokens, 2 * LANES), 0) % ROPE_TOKENS_PER_ROW
    slot_c = (lax.broadcasted_iota(jnp.int32, (tokens, 2 * LANES), 1) % LANES) // (ROPE_DIM // 2)
    own = jnp.where(tok_c == slot_c, by_tok, 0.0)
    out = sum(_dot(piece, spread_ref[...]) for piece in _split3(own))
    lane_in_head = lax.broadcasted_iota(jnp.int32, (1, LANES), 1) % SWA_HEAD_DIM
    cos_ref[...] = out[:, 0:LANES] + jnp.where(lane_in_head >= ROPE_DIM, 1.0, 0.0)
    sin_ref[...] = out[:, LANES:2 * LANES]


def _rope(x, cos, sin_signed, lower_half):
    half = ROPE_DIM // 2
    up = pltpu.roll(x, LANES - half, axis=1)
    down = pltpu.roll(x, half, axis=1)
    return x * cos + jnp.where(lower_half, up, down) * sin_signed


def _mixer_kernel(sinks_ref, x_ref, cos_ref, sin_ref, n1w_ref, wina_ref, winga_ref, winb_ref, convw_ref, convb_ref,
                  wa2_ref, ba_ref, onw_ref, qnw_ref, knw_ref, wbr_ref, wo_ref, out_ref,
                  conv_buf, gla_state, k_prev, v_prev, *, tok, layer):
    t = pl.program_id(1)

    @pl.when(t == 0)
    def _():
        conv_buf[0:SUBLANES, :] = jnp.zeros((SUBLANES, CONV_DIM), F32)
        gla_state[...] = jnp.zeros_like(gla_state)
        k_prev[...] = jnp.zeros_like(k_prev)
        v_prev[...] = jnp.zeros_like(v_prev)

    weights = (n1w_ref, wina_ref, winga_ref, winb_ref, convw_ref, convb_ref, wa2_ref, ba_ref, onw_ref, qnw_ref,
               knw_ref, wbr_ref, wo_ref)
    state = (conv_buf, gla_state, k_prev, v_prev)
    sinks = [sinks_ref[layer, h] * LOG2_E for h in range(SWA_HEADS)]
    for s in range(tok // SUB_TILE):
        rows = slice(s * SUB_TILE, (s + 1) * SUB_TILE)
        no_prev_bias = jnp.where(t == 0, -jnp.inf, 0.0) if s == 0 else None
        out_ref[0, rows, :] = _mixer_subtile(x_ref[0, rows, :], cos_ref[0, rows, :], sin_ref[0, rows, :],
                                             no_prev_bias, sinks, weights, state)


def _mixer_subtile(x, cos, sin_signed, no_prev_bias, sinks, weights, state):
    (n1w_ref, wina_ref, winga_ref, winb_ref, convw_ref, convb_ref, wa2_ref, ba_ref, onw_ref, qnw_ref,
     knw_ref, wbr_ref, wo_ref) = weights
    conv_buf, gla_state, k_prev, v_prev = state
    tok = SUB_TILE
    n_chunks = tok // GLA_CHUNK
    n_blocks = tok // SWA_BLOCK

    ub = (_rms_scale(x) * n1w_ref[...]).astype(BF16)

    gate_cols = 2 * LANES
    gate_pieces = []

    def gate_slices(count):
        for _ in range(count):
            c0 = B_GATE + len(gate_pieces) * gate_cols
            gate_pieces.append(jax.nn.sigmoid(_dot(ub, winb_ref[:, c0:c0 + gate_cols])))

    ga = _dot(ub, winga_ref[...])
    pg = _dot(ub, wina_ref[:, A_GQ:IN_A])
    gq = pg[:, 0:A_GK - A_GQ] * (GLA_DK ** -0.5)
    gk = pg[:, A_GK - A_GQ:A_GV - A_GQ]
    gv = pg[:, A_GV - A_GQ:A_GR - A_GQ]
    gr = pg[:, A_GR - A_GQ:IN_A - A_GQ]
    ps = _dot(ub, winb_ref[:, B_SQ:B_GATE])
    sq = ps[:, B_SQ:B_SK]
    sk = ps[:, B_SK:B_SV]
    sv = ps[:, B_SV:B_GATE]

    z = _dot(ga.astype(BF16), wa2_ref[...]) + ba_ref[...]
    glog = (jnp.minimum(z, 0.0) - jnp.log(1.0 + jnp.exp(-jnp.abs(z)))) * (1.0 / GLA_TAU)
    gate_slices(1)

    gi = lax.broadcasted_iota(jnp.int32, (2 * LANES, 2 * LANES), 0) // SWA_HEAD_DIM
    gj = lax.broadcasted_iota(jnp.int32, (2 * LANES, 2 * LANES), 1) // SWA_HEAD_DIM
    head_ones = jnp.where(gi == gj, 1.0, 0.0).astype(BF16)
    n_qcols = SWA_HEADS * SWA_HEAD_DIM // (2 * LANES)
    q_halves = [sq[:, j * 2 * LANES:(j + 1) * 2 * LANES] for j in range(n_qcols)]
    ssq = [_dot((qj * qj).astype(BF16), head_ones) for qj in q_halves]
    ssk = _dot((sk * sk).astype(BF16), head_ones[0:LANES, 0:LANES])

    cx = _dot(ub, wina_ref[:, A_CONV:A_CONV + CONV_DIM])
    cc = _dot(ub, wina_ref[:, A_CONV + 2 * CONV_DIM:A_CONV + 3 * CONV_DIM])
    gate_slices(2)
    cb = _dot(ub, wina_ref[:, A_CONV + CONV_DIM:A_CONV + 2 * CONV_DIM])
    uc = cc * cx
    conv_buf[SUBLANES:SUBLANES + tok, :] = uc
    u1 = conv_buf[SUBLANES - 1:SUBLANES - 1 + tok, :]
    u2 = conv_buf[SUBLANES - 2:SUBLANES - 2 + tok, :]
    yconv = convw_ref[0:1, :] * u2 + convw_ref[1:2, :] * u1 + convw_ref[2:3, :] * uc
    ya = (cb * (yconv + convb_ref[...])).astype(BF16)
    conv_buf[0:SUBLANES, :] = uc[tok - SUBLANES:tok, :]

    ri = lax.broadcasted_iota(jnp.int32, (tok, tok), 0)
    ci = lax.broadcasted_iota(jnp.int32, (tok, tok), 1)
    same_chunk = (ri // GLA_CHUNK) == (ci // GLA_CHUNK)
    causal_chunk = same_chunk & (ci <= ri)
    l_tri = jnp.where(causal_chunk, 1.0, 0.0).astype(BF16)
    b = _split_dot(l_tri, glog)
    b_last = jnp.concatenate(
        [jnp.broadcast_to(b[(c + 1) * GLA_CHUNK - 1:(c + 1) * GLA_CHUNK, :], (GLA_CHUNK, GLA_HEADS * GLA_DK))
         for c in range(n_chunks)], axis=0)
    gate_slices(2)
    q_in_b = (gq * jnp.exp(b)).astype(BF16)
    k_in = (gk * jnp.exp(-b)).astype(BF16)
    k_end_b = (gk * jnp.exp(b_last - b)).astype(BF16)
    decay = jnp.exp(b_last)
    head_masks = [_lane_group_mask(GLA_HEADS * GLA_DK, GLA_DK, h).astype(BF16) for h in range(GLA_HEADS)]
    gvb = gv.astype(BF16)

    lower_half = (lax.broadcasted_iota(jnp.int32, (1, LANES), 1) % SWA_HEAD_DIM) < ROPE_DIM // 2
    q_cols = []
    for j in range(n_qcols):
        qn = q_halves[j] * lax.rsqrt(ssq[j] * (1.0 / SWA_HEAD_DIM) + NORM_EPS)
        for i in range(2):
            qn_i = qn[:, i * LANES:(i + 1) * LANES] * qnw_ref[...]
            q_cols.append((_rope(qn_i, cos, sin_signed, lower_half) * (SWA_HEAD_DIM ** -0.5 * LOG2_E)).astype(BF16))
    kn = sk * lax.rsqrt(ssk * (1.0 / SWA_HEAD_DIM) + NORM_EPS) * knw_ref[...]
    kr = _rope(kn, cos, sin_signed, lower_half)

    upd = []
    for c in range(n_chunks):
        r0 = c * GLA_CHUNK
        ke_c = k_end_b[r0:r0 + GLA_CHUNK, :]
        k_stack = jnp.concatenate([ke_c * head_masks[h] for h in range(GLA_HEADS)], axis=0)
        v_stack = jnp.concatenate(
            [gv[r0:r0 + GLA_CHUNK, h * GLA_DV:(h + 1) * GLA_DV] for h in range(GLA_HEADS)], axis=0)
        upd.append(_dot(v_stack.T.astype(BF16), k_stack))

    group_masks = [_lane_group_mask(2 * LANES, SWA_HEAD_DIM, g).astype(BF16) for g in range(SWA_GROUP)]
    lane128 = lax.broadcasted_iota(jnp.int32, (1, LANES), 1)
    slot_j = lax.broadcasted_iota(jnp.int32, (SWA_BLOCK, SWA_GROUP * SWA_BLOCK), 0)
    qry_i = lax.broadcasted_iota(jnp.int32, (SWA_BLOCK, SWA_GROUP * SWA_BLOCK), 1) % SWA_BLOCK
    from_prev = slot_j > qry_i

    def kv_keys(k_blk):
        k_sw = pltpu.roll(k_blk, SWA_HEAD_DIM, axis=1)
        reps = []
        for kv in range(SWA_KV_HEADS):
            k_one = jnp.where((lane128 // SWA_HEAD_DIM) == kv, k_blk, k_sw).astype(BF16)
            reps.append(jnp.concatenate([k_one, k_one], axis=1))
        return reps

    keys = [kv_keys(k_prev[...])] + [kv_keys(kr[n * SWA_BLOCK:(n + 1) * SWA_BLOCK, :]) for n in range(n_blocks)]
    vt_f32 = [sv[n * SWA_BLOCK:(n + 1) * SWA_BLOCK, :].T for n in range(n_blocks)]
    vt = [v_prev[...].astype(BF16)] + [v.astype(BF16) for v in vt_f32]

    def swa_scores(n, kv):
        r0 = n * SWA_BLOCK
        q_kv = jnp.concatenate(q_cols[2 * kv:2 * kv + 2], axis=1)[r0:r0 + SWA_BLOCK, :]
        q_stack = jnp.concatenate([q_kv * group_masks[g] for g in range(SWA_GROUP)], axis=0)
        k_both = jnp.concatenate([keys[n][kv], keys[n + 1][kv]], axis=0)
        return _dot_nt(k_both, q_stack)

    def swa_softmax(n, kv, s_both):
        s_prev = s_both[0:SWA_BLOCK, :]
        if n == 0 and no_prev_bias is not None:
            s_prev = s_prev + no_prev_bias
        s_t = jnp.where(from_prev, s_prev, s_both[SWA_BLOCK:2 * SWA_BLOCK, :])
        sink = jnp.concatenate(
            [jnp.full((1, SWA_BLOCK), sinks[kv * SWA_GROUP + g], F32) for g in range(SWA_GROUP)], axis=1)
        m = jnp.maximum(jnp.max(s_t, axis=0, keepdims=True), sink)
        p_t = jnp.exp2(s_t - m)
        inv = 1.0 / (jnp.sum(p_t, axis=0, keepdims=True) + jnp.exp2(sink - m))
        p_both = jnp.concatenate([jnp.where(from_prev, p_t, 0.0), jnp.where(from_prev, 0.0, p_t)], axis=0)
        return p_both.astype(BF16), inv

    def swa_values(n, kv, p_both, inv):
        hd = slice(kv * SWA_HEAD_DIM, (kv + 1) * SWA_HEAD_DIM)
        vt_both = jnp.concatenate([vt[n][hd, :], vt[n + 1][hd, :]], axis=1)
        o_t = _dot(vt_both, p_both) * inv
        return [o_t[:, g * SWA_BLOCK:(g + 1) * SWA_BLOCK] for g in range(SWA_GROUP)]

    def gla_scores(h):
        att = _dot_nt(q_in_b * head_masks[h], k_in)
        return jnp.where(causal_chunk, att, 0.0).astype(BF16)

    def gla_values(h, att):
        return _dot(att, gvb[:, h * GLA_DV:(h + 1) * GLA_DV])

    steps = [(n, kv) for n in range(n_blocks) for kv in range(SWA_KV_HEADS)]
    assert len(steps) == 4 and GLA_HEADS == 4 and n_chunks == 4

    s0 = swa_scores(*steps[0])
    att0 = gla_scores(0)
    att1 = gla_scores(1)

    state = gla_state[...]
    o_inter = [[] for _ in range(GLA_HEADS)]
    for c in range(n_chunks):
        r0 = c * GLA_CHUNK
        q_c = q_in_b[r0:r0 + GLA_CHUNK, :]
        q_stack = jnp.concatenate([q_c * head_masks[h] for h in range(GLA_HEADS)], axis=0)
        oi = _dot_nt(q_stack, state.astype(BF16))
        for h in range(GLA_HEADS):
            o_inter[h].append(oi[h * GLA_CHUNK:(h + 1) * GLA_CHUNK, :])
        state = decay[r0:r0 + 1, :] * state + upd[c]
    gla_state[...] = state

    s1 = swa_scores(*steps[1])
    o_heads = swa_values(*steps[0], *swa_softmax(*steps[0], s0))
    ov0 = gla_values(0, att0)
    ov1 = gla_values(1, att1)
    att2 = gla_scores(2)
    att3 = gla_scores(3)
    yc_blocks = []
    s2 = swa_scores(*steps[2])
    o_heads += swa_values(*steps[1], *swa_softmax(*steps[1], s1))
    yc_blocks.append(jnp.concatenate(o_heads, axis=0).T)
    gate_slices(1)
    ov2 = gla_values(2, att2)
    ov3 = gla_values(3, att3)
    s3 = swa_scores(*steps[3])
    o_heads = swa_values(*steps[2], *swa_softmax(*steps[2], s2))
    gate_slices(2)
    o_heads += swa_values(*steps[3], *swa_softmax(*steps[3], s3))
    yc_blocks.append(jnp.concatenate(o_heads, axis=0).T)
    yc = jnp.concatenate(yc_blocks, axis=0).astype(BF16)
    k_prev[...] = kr[tok - SWA_BLOCK:tok, :]
    v_prev[...] = vt_f32[-1]

    br_a = _dot(ya, wbr_ref[0])
    gate_slices(3)

    yb_parts = []
    for h, ov in enumerate((ov0, ov1, ov2, ov3)):
        o_h = ov + jnp.concatenate(o_inter[h], axis=0)
        o_h = _rms_scale(o_h) * onw_ref[...]
        r_h = gr[:, h * GLA_DV:(h + 1) * GLA_DV]
        yb_parts.append((o_h * (r_h * jax.nn.sigmoid(r_h))).astype(BF16))
    yb = jnp.concatenate(yb_parts, axis=1)
    br_b = _dot(yb, wbr_ref[1])
    gate_slices(1)

    per_branch = D_MODEL // gate_cols
    assert len(gate_pieces) == N_BRANCHES * per_branch
    acc = x
    mixed_slices = []
    for j in range(per_branch):
        cols = slice(j * gate_cols, (j + 1) * gate_cols)
        br_c_j = _dot(yc, wbr_ref[2, :, cols])
        mixed_j = (gate_pieces[j] * br_a[:, cols] + gate_pieces[per_branch + j] * br_b[:, cols]
                   + gate_pieces[2 * per_branch + j] * br_c_j)
        mixed_slices.append(mixed_j.astype(BF16))
        if j >= 1:
            acc = acc + _dot(mixed_slices[j - 1], wo_ref[(j - 1) * gate_cols:j * gate_cols, :])
    return acc + _dot(mixed_slices[-1], wo_ref[(per_branch - 1) * gate_cols:per_branch * gate_cols, :])


def _ffn_kernel(x_ref, n2w_ref, wup_ref, wdown_ref, out_ref):
    x = x_ref[...]
    hb = (_rms_scale(x) * n2w_ref[...]).astype(BF16)
    acc = x
    for c in range(D_FF // FF_CHUNK):
        a = jnp.maximum(_dot(hb, wup_ref[:, c * FF_CHUNK:(c + 1) * FF_CHUNK]), 0.0)
        acc = acc + _dot((a * a).astype(BF16), wdown_ref[c * FF_CHUNK:(c + 1) * FF_CHUNK, :])
    out_ref[...] = acc


def _const_spec(shape):
    zeros = (0,) * len(shape)
    return pl.BlockSpec(shape, lambda *_: zeros, pipeline_mode=pl.Buffered(1))


def _layer_spec(layer, shape):
    zeros = (0,) * len(shape)
    return pl.BlockSpec((None,) + tuple(shape), lambda *_: (layer,) + zeros, pipeline_mode=pl.Buffered(1))


def _mixer_call(x, cos, sin, p, layer, tok):
    bsz, seq, d = x.shape
    grid = (bsz, seq // tok)
    tile = lambda b, t: (b, t, 0)
    spec = functools.partial(_layer_spec, layer)
    in_specs = [
        pl.BlockSpec(memory_space=pltpu.SMEM),
        pl.BlockSpec((1, tok, d), tile),
        pl.BlockSpec((1, tok, LANES), tile),
        pl.BlockSpec((1, tok, LANES), tile),
        spec((1, d)),
        spec((d, IN_A)),
        spec((d, GA_PAD)),
        spec((d, IN_B)),
        spec((CONV_WIDTH, CONV_DIM)),
        spec((1, CONV_DIM)),
        spec((GA_PAD, GLA_HEADS * GLA_DK)),
        spec((1, GLA_HEADS * GLA_DK)),
        spec((1, GLA_DV)),
        spec((1, LANES)),
        spec((1, LANES)),
        spec((N_BRANCHES, BRANCH_WIDTH, d)),
        spec((d, d)),
    ]
    scratch = [
        pltpu.VMEM((SUB_TILE + SUBLANES, CONV_DIM), F32),
        pltpu.VMEM((GLA_DV, GLA_HEADS * GLA_DK), F32),
        pltpu.VMEM((SWA_BLOCK, SWA_KV_HEADS * SWA_HEAD_DIM), F32),
        pltpu.VMEM((SWA_BLOCK, SWA_KV_HEADS * SWA_HEAD_DIM), F32),
    ]
    return pl.pallas_call(
        functools.partial(_mixer_kernel, tok=tok, layer=layer),
        grid=grid,
        in_specs=in_specs,
        out_specs=pl.BlockSpec((1, tok, d), tile),
        out_shape=jax.ShapeDtypeStruct(x.shape, F32),
        scratch_shapes=scratch,
        compiler_params=pltpu.CompilerParams(
            dimension_semantics=("arbitrary", "arbitrary"),
            vmem_limit_bytes=V7X_VMEM_BYTES * 3 // 4),
        name="mixer",
    )(p["sinks"], x, cos, sin, p["n1w"], p["w_in_a"], p["w_in_ga"], p["w_in_b"], p["conv_w"], p["conv_b"],
      p["wa2"], p["ba"], p["onw"], p["qnw"], p["knw"], p["w_branch"], p["w_o"])


def _ffn_call(x2, p, layer, tile):
    n, d = x2.shape
    spec = functools.partial(_layer_spec, layer)
    return pl.pallas_call(
        _ffn_kernel,
        grid=(n // tile,),
        in_specs=[
            pl.BlockSpec((tile, d), lambda i: (i, 0)),
            spec((1, d)),
            spec((d, D_FF)),
            spec((D_FF, d)),
        ],
        out_specs=pl.BlockSpec((tile, d), lambda i: (i, 0)),
        out_shape=jax.ShapeDtypeStruct(x2.shape, F32),
        compiler_params=pltpu.CompilerParams(
            dimension_semantics=("arbitrary",),
            vmem_limit_bytes=V7X_VMEM_BYTES * 3 // 4),
        name="ffn",
    )(x2, p["n2w"], p["w_up"], p["w_down"])


def _rope_spread_matrix():
    half = ROPE_DIM // 2
    c = np.arange(LANES)[:, None]
    l = np.arange(LANES)[None, :]
    hit = ((c % half) == (l % half)) & ((l % SWA_HEAD_DIM) < ROPE_DIM)
    sign = np.where((l % SWA_HEAD_DIM) < half, -1.0, 1.0)
    m = np.zeros((2 * LANES, 2 * LANES), np.float32)
    m[:LANES, :LANES] = hit
    m[LANES:, LANES:] = hit * sign
    return jnp.asarray(m, BF16)


def _rope_tables(positions):
    bsz, seq = positions.shape
    n = bsz * seq
    tokens = min(ROPE_TILE, n)
    assert n % tokens == 0 and tokens % (ROPE_TOKENS_PER_ROW * SUBLANES) == 0
    half = ROPE_DIM // 2
    inv_freq = ROPE_THETA ** (-jnp.arange(0, ROPE_DIM, 2, dtype=F32) / ROPE_DIM)
    invf_dense = jnp.tile(inv_freq, LANES // half)[None, :]
    pos_dense = jnp.repeat(positions.astype(jnp.int32).reshape(n), half).reshape(n // ROPE_TOKENS_PER_ROW, LANES)
    rows = tokens // ROPE_TOKENS_PER_ROW
    cos, sin = pl.pallas_call(
        functools.partial(_rope_table_kernel, tokens=tokens),
        grid=(n // tokens,),
        in_specs=[
            pl.BlockSpec((rows, LANES), lambda i: (i, 0)),
            _const_spec((1, LANES)),
            _const_spec((2 * LANES, 2 * LANES)),
        ],
        out_specs=[pl.BlockSpec((tokens, LANES), lambda i: (i, 0))] * 2,
        out_shape=[jax.ShapeDtypeStruct((n, LANES), F32)] * 2,
        compiler_params=pltpu.CompilerParams(dimension_semantics=("arbitrary",)),
        name="rope_tables",
    )(pos_dense, invf_dense, _rope_spread_matrix())
    return cos.reshape(bsz, seq, LANES), sin.reshape(bsz, seq, LANES)


def _prepare_params(norm1_w, w_in, conv_w, conv_b, gla_wa2, gla_ba, gla_onorm_w, q_norm_w, k_norm_w,
                    sinks, w_branch, w_o, norm2_w, w_up, w_down):
    heads_per_tile = LANES // SWA_HEAD_DIM
    w_in_b16 = w_in.astype(BF16)
    return {
        "sinks": sinks.astype(F32),
        "n1w": norm1_w[:, None, :],
        "w_in_a": w_in_b16[:, :, :IN_A],
        "w_in_ga": jnp.pad(w_in_b16[:, :, IN_A:IN_A + GLA_RANK], ((0, 0), (0, 0), (0, GA_PAD - GLA_RANK))),
        "w_in_b": w_in_b16[:, :, IN_A + GLA_RANK:],
        "conv_w": conv_w,
        "conv_b": conv_b[:, None, :],
        "wa2": jnp.pad(gla_wa2, ((0, 0), (0, GA_PAD - GLA_RANK), (0, 0))).astype(BF16),
        "ba": gla_ba[:, None, :],
        "onw": gla_onorm_w[:, None, :],
        "qnw": jnp.tile(q_norm_w, (1, heads_per_tile))[:, None, :],
        "knw": jnp.tile(k_norm_w, (1, heads_per_tile))[:, None, :],
        "w_branch": w_branch.astype(BF16),
        "w_o": w_o.astype(BF16),
        "n2w": norm2_w[:, None, :],
        "w_up": w_up.astype(BF16),
        "w_down": w_down.astype(BF16),
    }


def kernel(x, positions, norm1_w, w_in, conv_w, conv_b, gla_wa2, gla_ba, gla_onorm_w, q_norm_w, k_norm_w,
           sinks, w_branch, w_o, norm2_w, w_up, w_down):
    bsz, seq, d = x.shape
    assert d == D_MODEL and w_in.shape[2] == N_IN
    tok = min(TOK_TILE, seq)
    assert seq % tok == 0 and tok % SUB_TILE == 0 and SUB_TILE % (2 * SWA_BLOCK) == 0
    ffn_tile = min(FFN_TILE, bsz * seq)
    assert (bsz * seq) % ffn_tile == 0
    cos, sin = _rope_tables(positions)
    p = _prepare_params(norm1_w, w_in, conv_w, conv_b, gla_wa2, gla_ba, gla_onorm_w, q_norm_w, k_norm_w,
                        sinks, w_branch, w_o, norm2_w, w_up, w_down)
    for layer in range(w_in.shape[0]):
        x = _mixer_call(x, cos, sin, p, layer, tok)
        x = _ffn_call(x.reshape(bsz * seq, d), p, layer, ffn_tile).reshape(bsz, seq, d)
    return x
```

```python
import functools

import jax
import jax.numpy as jnp
import numpy as np
from jax import lax
from jax.experimental import pallas as pl
from jax.experimental.pallas import tpu as pltpu

F32 = jnp.float32
BF16 = jnp.bfloat16

D_MODEL = 1024
CONV_DIM = 512
CONV_WIDTH = 3
GLA_HEADS = 4
GLA_DK = 64
GLA_DV = 128
GLA_RANK = 16
GLA_TAU = 16.0
GLA_CHUNK = 64
SWA_HEADS = 8
SWA_KV_HEADS = 2
SWA_GROUP = SWA_HEADS // SWA_KV_HEADS
SWA_HEAD_DIM = 64
SWA_BLOCK = 128
ROPE_THETA = 500000.0
ROPE_DIM = SWA_HEAD_DIM // 4
N_BRANCHES = 3
BRANCH_WIDTH = 512
D_FF = 4 * D_MODEL
NORM_EPS = 1e-6
LOG2_E = 1.4426950408889634

LANES = 128
SUBLANES = 8
V7X_VMEM_BYTES = 64 * 1024 * 1024

GA_PAD = LANES
IN_A = 3 * CONV_DIM + 2 * GLA_HEADS * GLA_DK + 2 * GLA_HEADS * GLA_DV
IN_B = SWA_HEADS * SWA_HEAD_DIM + 2 * SWA_KV_HEADS * SWA_HEAD_DIM + N_BRANCHES * D_MODEL
N_IN = IN_A + GLA_RANK + IN_B
A_CONV = 0
A_GQ = A_CONV + 3 * CONV_DIM
A_GK = A_GQ + GLA_HEADS * GLA_DK
A_GV = A_GK + GLA_HEADS * GLA_DK
A_GR = A_GV + GLA_HEADS * GLA_DV
B_SQ = 0
B_SK = B_SQ + SWA_HEADS * SWA_HEAD_DIM
B_SV = B_SK + SWA_KV_HEADS * SWA_HEAD_DIM
B_GATE = B_SV + SWA_KV_HEADS * SWA_HEAD_DIM

ROPE_TOKENS_PER_ROW = LANES // (ROPE_DIM // 2)
ROPE_TILE = 1024
SUB_TILE = 256
TOK_TILE = 512
FFN_TILE = 1024
FF_CHUNK = 1024


def _dot(a, b):
    return jnp.dot(a, b, preferred_element_type=F32)


def _dot_nt(a, b):
    return lax.dot_general(a, b, (((1,), (1,)), ((), ())), preferred_element_type=F32)


def _split_dot(m_bf16, v_f32):
    hi = v_f32.astype(BF16)
    lo = (v_f32 - hi.astype(F32)).astype(BF16)
    return _dot(m_bf16, hi) + _dot(m_bf16, lo)


def _split_dot_rhs(v_f32, m_bf16):
    hi = v_f32.astype(BF16)
    lo = (v_f32 - hi.astype(F32)).astype(BF16)
    return _dot(hi, m_bf16) + _dot(lo, m_bf16)


def _rms_scale(x):
    return x * lax.rsqrt(jnp.mean(x * x, axis=-1, keepdims=True) + NORM_EPS)


def _lane_group_mask(width, group, index):
    lane = lax.broadcasted_iota(jnp.int32, (1, width), 1)
    return jnp.where((lane // group) == index, 1.0, 0.0).astype(F32)


def _split3(v_f32):
    hi = v_f32.astype(BF16)
    r1 = v_f32 - hi.astype(F32)
    mid = r1.astype(BF16)
    lo = (r1 - mid.astype(F32)).astype(BF16)
    return hi, mid, lo


def _rope_table_kernel(pos_ref, invf_ref, spread_ref, cos_ref, sin_ref, *, tokens):
    rows = tokens // ROPE_TOKENS_PER_ROW
    ang = pos_ref[...].astype(F32) * invf_ref[...]
    dense = jnp.concatenate([jnp.cos(ang), jnp.sin(ang)], axis=1)
    by_tok = jnp.broadcast_to(dense[:, None, :], (rows, ROPE_TOKENS_PER_ROW, 2 * LANES)).reshape(tokens, 2 * LANES)
    tok_c = lax.broadcasted_iota(jnp.int32, (tokens, 2 * LANES), 0) % ROPE_TOKENS_PER_ROW
    slot_c = (lax.broadcasted_iota(jnp.int32, (tokens, 2 * LANES), 1) % LANES) // (ROPE_DIM // 2)
    own = jnp.where(tok_c == slot_c, by_tok, 0.0)
    out = sum(_dot(piece, spread_ref[...]) for piece in _split3(own))
    lane_in_head = lax.broadcasted_iota(jnp.int32, (1, LANES), 1) % SWA_HEAD_DIM
    cos_ref[...] = out[:, 0:LANES] + jnp.where(lane_in_head >= ROPE_DIM, 1.0, 0.0)
    sin_ref[...] = out[:, LANES:2 * LANES]


def _rope(x, cos, sin_signed, lower_half):
    half = ROPE_DIM // 2
    up = pltpu.roll(x, LANES - half, axis=1)
    down = pltpu.roll(x, half, axis=1)
    return x * cos + jnp.where(lower_half, up, down) * sin_signed


def _mixer_kernel(sinks_ref, x_ref, cos_ref, sin_ref, n1w_ref, wina_ref, winga_ref, winb_ref, convw_ref, convb_ref,
                  wa2_ref, ba_ref, onw_ref, qnw_ref, knw_ref, wbr_ref, wo_ref, wup_f32_ref, wdown_f32_ref,
                  out_ref, wup_ref, wdown_ref, conv_buf, gla_state, k_prev, v_prev, *, tok):
    t = pl.program_id(1)

    @pl.when(t == 0)
    def _():
        conv_buf[0:SUBLANES, :] = jnp.zeros((SUBLANES, CONV_DIM), F32)
        gla_state[...] = jnp.zeros_like(gla_state)
        k_prev[...] = jnp.zeros_like(k_prev)
        v_prev[...] = jnp.zeros_like(v_prev)

    weights = (n1w_ref, wina_ref, winga_ref, winb_ref, convw_ref, convb_ref, wa2_ref, ba_ref, onw_ref, qnw_ref,
               knw_ref, wbr_ref, wo_ref)
    state = (conv_buf, gla_state, k_prev, v_prev)
    sinks = [sinks_ref[h] * LOG2_E for h in range(SWA_HEADS)]
    for s in range(tok // SUB_TILE):
        rows = slice(s * SUB_TILE, (s + 1) * SUB_TILE)
        no_prev_bias = jnp.where(t == 0, -jnp.inf, 0.0) if s == 0 else None
        out_ref[0, rows, :] = _mixer_subtile(x_ref[0, rows, :], cos_ref[0, rows, :], sin_ref[0, rows, :],
                                             no_prev_bias, sinks, weights, state)
    wup_ref[...] = wup_f32_ref[...].astype(BF16)
    wdown_ref[...] = wdown_f32_ref[...].astype(BF16)


def _mixer_subtile(x, cos, sin_signed, no_prev_bias, sinks, weights, state):
    (n1w_ref, wina_ref, winga_ref, winb_ref, convw_ref, convb_ref, wa2_ref, ba_ref, onw_ref, qnw_ref,
     knw_ref, wbr_ref, wo_ref) = weights
    conv_buf, gla_state, k_prev, v_prev = state
    tok = SUB_TILE
    n_chunks = tok // GLA_CHUNK
    n_blocks = tok // SWA_BLOCK

    ub = (_rms_scale(x) * n1w_ref[...]).astype(BF16)

    gate_cols = 2 * LANES
    gate_pieces = []

    def gate_slices(count):
        for _ in range(count):
            c0 = B_GATE + len(gate_pieces) * gate_cols
            gate_pieces.append(jax.nn.sigmoid(_dot(ub, winb_ref[:, c0:c0 + gate_cols])))

    ga = _dot(ub, winga_ref[...])
    pg = _dot(ub, wina_ref[:, A_GQ:IN_A])
    gq = pg[:, 0:A_GK - A_GQ] * (GLA_DK ** -0.5)
    gk = pg[:, A_GK - A_GQ:A_GV - A_GQ]
    gv = pg[:, A_GV - A_GQ:A_GR - A_GQ]
    gr = pg[:, A_GR - A_GQ:IN_A - A_GQ]
    ps = _dot(ub, winb_ref[:, B_SQ:B_GATE])
    sq = ps[:, B_SQ:B_SK]
    sk = ps[:, B_SK:B_SV]
    sv = ps[:, B_SV:B_GATE]

    z = _dot(ga.astype(BF16), wa2_ref[...]) + ba_ref[...]
    glog = (jnp.minimum(z, 0.0) - jnp.log(1.0 + jnp.exp(-jnp.abs(z)))) * (1.0 / GLA_TAU)
    gate_slices(1)

    gi = lax.broadcasted_iota(jnp.int32, (2 * LANES, 2 * LANES), 0) // SWA_HEAD_DIM
    gj = lax.broadcasted_iota(jnp.int32, (2 * LANES, 2 * LANES), 1) // SWA_HEAD_DIM
    head_ones = jnp.where(gi == gj, 1.0, 0.0).astype(BF16)
    n_qcols = SWA_HEADS * SWA_HEAD_DIM // (2 * LANES)
    q_halves = [sq[:, j * 2 * LANES:(j + 1) * 2 * LANES] for j in range(n_qcols)]
    ssq = [_dot((qj * qj).astype(BF16), head_ones) for qj in q_halves]
    ssk = _dot((sk * sk).astype(BF16), head_ones[0:LANES, 0:LANES])

    cx = _dot(ub, wina_ref[:, A_CONV:A_CONV + CONV_DIM])
    cc = _dot(ub, wina_ref[:, A_CONV + 2 * CONV_DIM:A_CONV + 3 * CONV_DIM])
    gate_slices(2)
    cb = _dot(ub, wina_ref[:, A_CONV + CONV_DIM:A_CONV + 2 * CONV_DIM])
    uc = cc * cx
    conv_buf[SUBLANES:SUBLANES + tok, :] = uc
    u1 = conv_buf[SUBLANES - 1:SUBLANES - 1 + tok, :]
    u2 = conv_buf[SUBLANES - 2:SUBLANES - 2 + tok, :]
    yconv = convw_ref[0:1, :] * u2 + convw_ref[1:2, :] * u1 + convw_ref[2:3, :] * uc
    ya = (cb * (yconv + convb_ref[...])).astype(BF16)
    conv_buf[0:SUBLANES, :] = uc[tok - SUBLANES:tok, :]

    ri = lax.broadcasted_iota(jnp.int32, (tok, tok), 0)
    ci = lax.broadcasted_iota(jnp.int32, (tok, tok), 1)
    same_chunk = (ri // GLA_CHUNK) == (ci // GLA_CHUNK)
    causal_chunk = same_chunk & (ci <= ri)
    l_tri = jnp.where(causal_chunk, 1.0, 0.0).astype(BF16)
    b = _split_dot(l_tri, glog)
    b_last = jnp.concatenate(
        [jnp.broadcast_to(b[(c + 1) * GLA_CHUNK - 1:(c + 1) * GLA_CHUNK, :], (GLA_CHUNK, GLA_HEADS * GLA_DK))
         for c in range(n_chunks)], axis=0)
    gate_slices(2)
    q_in_b = (gq * jnp.exp(b)).astype(BF16)
    k_in = (gk * jnp.exp(-b)).astype(BF16)
    k_end_b = (gk * jnp.exp(b_last - b)).astype(BF16)
    decay = jnp.exp(b_last)
    head_masks = [_lane_group_mask(GLA_HEADS * GLA_DK, GLA_DK, h).astype(BF16) for h in range(GLA_HEADS)]
    gvb = gv.astype(BF16)

    lower_half = (lax.broadcasted_iota(jnp.int32, (1, LANES), 1) % SWA_HEAD_DIM) < ROPE_DIM // 2
    q_cols = []
    for j in range(n_qcols):
        qn = q_halves[j] * lax.rsqrt(ssq[j] * (1.0 / SWA_HEAD_DIM) + NORM_EPS)
        for i in range(2):
            qn_i = qn[:, i * LANES:(i + 1) * LANES] * qnw_ref[...]
            q_cols.append((_rope(qn_i, cos, sin_signed, lower_half) * (SWA_HEAD_DIM ** -0.5 * LOG2_E)).astype(BF16))
    kn = sk * lax.rsqrt(ssk * (1.0 / SWA_HEAD_DIM) + NORM_EPS) * knw_ref[...]
    kr = _rope(kn, cos, sin_signed, lower_half)

    upd = []
    for c in range(n_chunks):
        r0 = c * GLA_CHUNK
        ke_c = k_end_b[r0:r0 + GLA_CHUNK, :]
        k_stack = jnp.concatenate([ke_c * head_masks[h] for h in range(GLA_HEADS)], axis=0)
        v_stack = jnp.concatenate(
            [gv[r0:r0 + GLA_CHUNK, h * GLA_DV:(h + 1) * GLA_DV] for h in range(GLA_HEADS)], axis=0)
        upd.append(_dot(v_stack.T.astype(BF16), k_stack))

    group_masks = [_lane_group_mask(2 * LANES, SWA_HEAD_DIM, g).astype(BF16) for g in range(SWA_GROUP)]
    lane128 = lax.broadcasted_iota(jnp.int32, (1, LANES), 1)
    slot_j = lax.broadcasted_iota(jnp.int32, (SWA_BLOCK, SWA_GROUP * SWA_BLOCK), 0)
    qry_i = lax.broadcasted_iota(jnp.int32, (SWA_BLOCK, SWA_GROUP * SWA_BLOCK), 1) % SWA_BLOCK
    from_prev = slot_j > qry_i

    def kv_keys(k_blk):
        k_sw = pltpu.roll(k_blk, SWA_HEAD_DIM, axis=1)
        reps = []
        for kv in range(SWA_KV_HEADS):
            k_one = jnp.where((lane128 // SWA_HEAD_DIM) == kv, k_blk, k_sw).astype(BF16)
            reps.append(jnp.concatenate([k_one, k_one], axis=1))
        return reps

    keys = [kv_keys(k_prev[...])] + [kv_keys(kr[n * SWA_BLOCK:(n + 1) * SWA_BLOCK, :]) for n in range(n_blocks)]
    vt_f32 = [sv[n * SWA_BLOCK:(n + 1) * SWA_BLOCK, :].T for n in range(n_blocks)]
    vt = [v_prev[...].astype(BF16)] + [v.astype(BF16) for v in vt_f32]

    def swa_scores(n, kv):
        r0 = n * SWA_BLOCK
        q_kv = jnp.concatenate(q_cols[2 * kv:2 * kv + 2], axis=1)[r0:r0 + SWA_BLOCK, :]
        q_stack = jnp.concatenate([q_kv * group_masks[g] for g in range(SWA_GROUP)], axis=0)
        k_both = jnp.concatenate([keys[n][kv], keys[n + 1][kv]], axis=0)
        return _dot_nt(k_both, q_stack)

    def swa_softmax(n, kv, s_both):
        s_prev = s_both[0:SWA_BLOCK, :]
        if n == 0 and no_prev_bias is not None:
            s_prev = s_prev + no_prev_bias
        s_t = jnp.where(from_prev, s_prev, s_both[SWA_BLOCK:2 * SWA_BLOCK, :])
        sink = jnp.concatenate(
            [jnp.full((1, SWA_BLOCK), sinks[kv * SWA_GROUP + g], F32) for g in range(SWA_GROUP)], axis=1)
        m = jnp.maximum(jnp.max(s_t, axis=0, keepdims=True), sink)
        p_t = jnp.exp2(s_t - m)
        inv = 1.0 / (jnp.sum(p_t, axis=0, keepdims=True) + jnp.exp2(sink - m))
        p_both = jnp.concatenate([jnp.where(from_prev, p_t, 0.0), jnp.where(from_prev, 0.0, p_t)], axis=0)
        return p_both.astype(BF16), inv

    def swa_values(n, kv, p_both, inv):
        hd = slice(kv * SWA_HEAD_DIM, (kv + 1) * SWA_HEAD_DIM)
        vt_both = jnp.concatenate([vt[n][hd, :], vt[n + 1][hd, :]], axis=1)
        o_t = _dot(vt_both, p_both) * inv
        return [o_t[:, g * SWA_BLOCK:(g + 1) * SWA_BLOCK] for g in range(SWA_GROUP)]

    def gla_scores(h):
        att = _dot_nt(q_in_b * head_masks[h], k_in)
        return jnp.where(causal_chunk, att, 0.0).astype(BF16)

    def gla_values(h, att):
        return _dot(att, gvb[:, h * GLA_DV:(h + 1) * GLA_DV])

    steps = [(n, kv) for n in range(n_blocks) for kv in range(SWA_KV_HEADS)]
    assert len(steps) == 4 and GLA_HEADS == 4 and n_chunks == 4

    s0 = swa_scores(*steps[0])
    att0 = gla_scores(0)
    att1 = gla_scores(1)

    state = gla_state[...]
    o_inter = [[] for _ in range(GLA_HEADS)]
    for c in range(n_chunks):
        r0 = c * GLA_CHUNK
        q_c = q_in_b[r0:r0 + GLA_CHUNK, :]
        q_stack = jnp.concatenate([q_c * head_masks[h] for h in range(GLA_HEADS)], axis=0)
        oi = _dot_nt(q_stack, state.astype(BF16))
        for h in range(GLA_HEADS):
            o_inter[h].append(oi[h * GLA_CHUNK:(h + 1) * GLA_CHUNK, :])
        state = decay[r0:r0 + 1, :] * state + upd[c]
    gla_state[...] = state

    s1 = swa_scores(*steps[1])
    o_heads = swa_values(*steps[0], *swa_softmax(*steps[0], s0))
    ov0 = gla_values(0, att0)
    ov1 = gla_values(1, att1)
    att2 = gla_scores(2)
    att3 = gla_scores(3)
    yc_blocks = []
    s2 = swa_scores(*steps[2])
    o_heads += swa_values(*steps[1], *swa_softmax(*steps[1], s1))
    yc_blocks.append(jnp.concatenate(o_heads, axis=0).T)
    gate_slices(1)
    ov2 = gla_values(2, att2)
    ov3 = gla_values(3, att3)
    s3 = swa_scores(*steps[3])
    o_heads = swa_values(*steps[2], *swa_softmax(*steps[2], s2))
    gate_slices(2)
    o_heads += swa_values(*steps[3], *swa_softmax(*steps[3], s3))
    yc_blocks.append(jnp.concatenate(o_heads, axis=0).T)
    yc = jnp.concatenate(yc_blocks, axis=0).astype(BF16)
    k_prev[...] = kr[tok - SWA_BLOCK:tok, :]
    v_prev[...] = vt_f32[-1]

    br_a = _dot(ya, wbr_ref[0])
    gate_slices(3)

    yb_parts = []
    for h, ov in enumerate((ov0, ov1, ov2, ov3)):
        o_h = ov + jnp.concatenate(o_inter[h], axis=0)
        o_h = _rms_scale(o_h) * onw_ref[...]
        r_h = gr[:, h * GLA_DV:(h + 1) * GLA_DV]
        yb_parts.append((o_h * (r_h * jax.nn.sigmoid(r_h))).astype(BF16))
    yb = jnp.concatenate(yb_parts, axis=1)
    br_b = _dot(yb, wbr_ref[1])
    gate_slices(1)

    per_branch = D_MODEL // gate_cols
    assert len(gate_pieces) == N_BRANCHES * per_branch
    acc = x
    mixed_slices = []
    for j in range(per_branch):
        cols = slice(j * gate_cols, (j + 1) * gate_cols)
        br_c_j = _dot(yc, wbr_ref[2, :, cols])
        mixed_j = (gate_pieces[j] * br_a[:, cols] + gate_pieces[per_branch + j] * br_b[:, cols]
                   + gate_pieces[2 * per_branch + j] * br_c_j)
        mixed_slices.append(mixed_j.astype(BF16))
        if j >= 1:
            acc = acc + _dot(mixed_slices[j - 1], wo_ref[(j - 1) * gate_cols:j * gate_cols, :])
    return acc + _dot(mixed_slices[-1], wo_ref[(per_branch - 1) * gate_cols:per_branch * gate_cols, :])


def _ffn_kernel(x_ref, n2w_ref, wup_ref, wdown_ref, *rest):
    if len(rest) == 1:
        (out_ref,) = rest
    else:
        win_f32_ref, wbr_f32_ref, wo_f32_ref, out_ref, wina_ref, winga_ref, winb_ref, wbr_ref, wo_ref = rest
        rows = win_f32_ref.shape[0]
        wina_ref[...] = win_f32_ref[:, 0:IN_A].astype(BF16)
        lane = lax.broadcasted_iota(jnp.int32, (rows, GA_PAD), 1)
        winga_ref[...] = jnp.where(lane < GLA_RANK, win_f32_ref[:, IN_A:IN_A + GA_PAD], 0.0).astype(BF16)
        winb_ref[...] = win_f32_ref[:, IN_A + GLA_RANK:N_IN].astype(BF16)
        wbr_ref[...] = wbr_f32_ref[...].astype(BF16)
        wo_ref[...] = wo_f32_ref[...].astype(BF16)
    x = x_ref[...]
    hb = (_rms_scale(x) * n2w_ref[...]).astype(BF16)
    acc = x
    for c in range(D_FF // FF_CHUNK):
        a = jnp.maximum(_dot(hb, wup_ref[:, c * FF_CHUNK:(c + 1) * FF_CHUNK]), 0.0)
        acc = acc + _dot((a * a).astype(BF16), wdown_ref[c * FF_CHUNK:(c + 1) * FF_CHUNK, :])
    out_ref[...] = acc


def _const_spec(shape):
    zeros = (0,) * len(shape)
    return pl.BlockSpec(shape, lambda *_: zeros, pipeline_mode=pl.Buffered(1))


def _row_block(n_rows, n_steps, dtype):
    rows = n_rows // n_steps
    assert rows * n_steps == n_rows and rows % (2 * SUBLANES if dtype == BF16 else SUBLANES) == 0
    return rows


def _mixer_call(x, cos, sin, p, w_up_f32, w_down_f32, tok):
    bsz, seq, d = x.shape
    n_t = seq // tok
    grid = (bsz, n_t)
    n_steps = bsz * n_t
    tile = lambda b, t: (b, t, 0)
    step_rows = lambda b, t: (b * n_t + t, 0)
    up_rows = _row_block(d, n_steps, BF16)
    down_rows = _row_block(D_FF, n_steps, BF16)
    spec = _const_spec
    in_specs = [
        pl.BlockSpec(memory_space=pltpu.SMEM),
        pl.BlockSpec((1, tok, d), tile),
        pl.BlockSpec((1, tok, LANES), tile),
        pl.BlockSpec((1, tok, LANES), tile),
        spec((1, d)),
        spec((d, IN_A)),
        spec((d, GA_PAD)),
        spec((d, IN_B)),
        spec((CONV_WIDTH, CONV_DIM)),
        spec((1, CONV_DIM)),
        spec((GA_PAD, GLA_HEADS * GLA_DK)),
        spec((1, GLA_HEADS * GLA_DK)),
        spec((1, GLA_DV)),
        spec((1, LANES)),
        spec((1, LANES)),
        spec((N_BRANCHES, BRANCH_WIDTH, d)),
        spec((d, d)),
        pl.BlockSpec((up_rows, D_FF), step_rows),
        pl.BlockSpec((down_rows, d), step_rows),
    ]
    out_specs = [
        pl.BlockSpec((1, tok, d), tile),
        pl.BlockSpec((up_rows, D_FF), step_rows),
        pl.BlockSpec((down_rows, d), step_rows),
    ]
    out_shape = [
        jax.ShapeDtypeStruct(x.shape, F32),
        jax.ShapeDtypeStruct((d, D_FF), BF16),
        jax.ShapeDtypeStruct((D_FF, d), BF16),
    ]
    scratch = [
        pltpu.VMEM((SUB_TILE + SUBLANES, CONV_DIM), F32),
        pltpu.VMEM((GLA_DV, GLA_HEADS * GLA_DK), F32),
        pltpu.VMEM((SWA_BLOCK, SWA_KV_HEADS * SWA_HEAD_DIM), F32),
        pltpu.VMEM((SWA_BLOCK, SWA_KV_HEADS * SWA_HEAD_DIM), F32),
    ]
    out, w_up_b16, w_down_b16 = pl.pallas_call(
        functools.partial(_mixer_kernel, tok=tok),
        grid=grid,
        in_specs=in_specs,
        out_specs=out_specs,
        out_shape=out_shape,
        scratch_shapes=scratch,
        compiler_params=pltpu.CompilerParams(
            dimension_semantics=("arbitrary", "arbitrary"),
            vmem_limit_bytes=V7X_VMEM_BYTES * 3 // 4),
        name="mixer",
    )(p["sinks"], x, cos, sin, p["n1w"], p["w_in_a"], p["w_in_ga"], p["w_in_b"], p["conv_w"], p["conv_b"],
      p["wa2"], p["ba"], p["onw"], p["qnw"], p["knw"], p["w_branch"], p["w_o"], w_up_f32, w_down_f32)
    return out, w_up_b16, w_down_b16


def _ffn_call(x2, n2w, w_up, w_down, tile, next_mixer_f32=None):
    n, d = x2.shape
    n_steps = n // tile
    step_rows = lambda i: (i, 0)
    in_specs = [
        pl.BlockSpec((tile, d), step_rows),
        _const_spec((1, d)),
        _const_spec((d, D_FF)),
        _const_spec((D_FF, d)),
    ]
    out_specs = [pl.BlockSpec((tile, d), step_rows)]
    out_shape = [jax.ShapeDtypeStruct(x2.shape, F32)]
    operands = [x2, n2w, w_up, w_down]
    if next_mixer_f32 is not None:
        w_in_f32, w_branch_f32, w_o_f32 = next_mixer_f32
        w_branch_f32 = w_branch_f32.reshape(N_BRANCHES * BRANCH_WIDTH, d)
        in_rows = _row_block(d, n_steps, BF16)
        br_rows = _row_block(N_BRANCHES * BRANCH_WIDTH, n_steps, BF16)
        in_specs += [
            pl.BlockSpec((in_rows, N_IN), step_rows),
            pl.BlockSpec((br_rows, d), step_rows),
            pl.BlockSpec((in_rows, d), step_rows),
        ]
        out_specs += [
            pl.BlockSpec((in_rows, IN_A), step_rows),
            pl.BlockSpec((in_rows, GA_PAD), step_rows),
            pl.BlockSpec((in_rows, IN_B), step_rows),
            pl.BlockSpec((br_rows, d), step_rows),
            pl.BlockSpec((in_rows, d), step_rows),
        ]
        out_shape += [
            jax.ShapeDtypeStruct((d, IN_A), BF16),
            jax.ShapeDtypeStruct((d, GA_PAD), BF16),
            jax.ShapeDtypeStruct((d, IN_B), BF16),
            jax.ShapeDtypeStruct((N_BRANCHES * BRANCH_WIDTH, d), BF16),
            jax.ShapeDtypeStruct((d, d), BF16),
        ]
        operands += [w_in_f32, w_branch_f32, w_o_f32]
    outs = pl.pallas_call(
        _ffn_kernel,
        grid=(n_steps,),
        in_specs=in_specs,
        out_specs=out_specs,
        out_shape=out_shape,
        compiler_params=pltpu.CompilerParams(
            dimension_semantics=("arbitrary",),
            vmem_limit_bytes=V7X_VMEM_BYTES * 3 // 4),
        name="ffn",
    )(*operands)
    if next_mixer_f32 is None:
        return outs[0], None
    y, w_in_a, w_in_ga, w_in_b, w_branch, w_o = outs
    return y, {"w_in_a": w_in_a, "w_in_ga": w_in_ga, "w_in_b": w_in_b,
               "w_branch": w_branch.reshape(N_BRANCHES, BRANCH_WIDTH, d), "w_o": w_o}


def _rope_spread_matrix():
    half = ROPE_DIM // 2
    c = np.arange(LANES)[:, None]
    l = np.arange(LANES)[None, :]
    hit = ((c % half) == (l % half)) & ((l % SWA_HEAD_DIM) < ROPE_DIM)
    sign = np.where((l % SWA_HEAD_DIM) < half, -1.0, 1.0)
    m = np.zeros((2 * LANES, 2 * LANES), np.float32)
    m[:LANES, :LANES] = hit
    m[LANES:, LANES:] = hit * sign
    return jnp.asarray(m, BF16)


def _rope_tables(positions):
    bsz, seq = positions.shape
    n = bsz * seq
    tokens = min(ROPE_TILE, n)
    assert n % tokens == 0 and tokens % (ROPE_TOKENS_PER_ROW * SUBLANES) == 0
    half = ROPE_DIM // 2
    inv_freq = ROPE_THETA ** (-jnp.arange(0, ROPE_DIM, 2, dtype=F32) / ROPE_DIM)
    invf_dense = jnp.tile(inv_freq, LANES // half)[None, :]
    pos_dense = jnp.repeat(positions.astype(jnp.int32).reshape(n), half).reshape(n // ROPE_TOKENS_PER_ROW, LANES)
    rows = tokens // ROPE_TOKENS_PER_ROW
    cos, sin = pl.pallas_call(
        functools.partial(_rope_table_kernel, tokens=tokens),
        grid=(n // tokens,),
        in_specs=[
            pl.BlockSpec((rows, LANES), lambda i: (i, 0)),
            _const_spec((1, LANES)),
            _const_spec((2 * LANES, 2 * LANES)),
        ],
        out_specs=[pl.BlockSpec((tokens, LANES), lambda i: (i, 0))] * 2,
        out_shape=[jax.ShapeDtypeStruct((n, LANES), F32)] * 2,
        compiler_params=pltpu.CompilerParams(dimension_semantics=("arbitrary",)),
        name="rope_tables",
    )(pos_dense, invf_dense, _rope_spread_matrix())
    return cos.reshape(bsz, seq, LANES), sin.reshape(bsz, seq, LANES)


def _first_mixer_weights(w_in, w_branch, w_o):
    w = w_in[0].astype(BF16)
    return {
        "w_in_a": w[:, :IN_A],
        "w_in_ga": jnp.pad(w[:, IN_A:IN_A + GLA_RANK], ((0, 0), (0, GA_PAD - GLA_RANK))),
        "w_in_b": w[:, IN_A + GLA_RANK:],
        "w_branch": w_branch[0].astype(BF16),
        "w_o": w_o[0].astype(BF16),
    }


def _small_mixer_params(layer, norm1_w, conv_w, conv_b, gla_wa2, gla_ba, gla_onorm_w, q_norm_w, k_norm_w, sinks):
    heads_per_tile = LANES // SWA_HEAD_DIM
    return {
        "sinks": sinks[layer].astype(F32),
        "n1w": norm1_w[layer][None, :],
        "conv_w": conv_w[layer],
        "conv_b": conv_b[layer][None, :],
        "wa2": jnp.pad(gla_wa2[layer], ((0, GA_PAD - GLA_RANK), (0, 0))).astype(BF16),
        "ba": gla_ba[layer][None, :],
        "onw": gla_onorm_w[layer][None, :],
        "qnw": jnp.tile(q_norm_w[layer], heads_per_tile)[None, :],
        "knw": jnp.tile(k_norm_w[layer], heads_per_tile)[None, :],
    }


def kernel(x, positions, norm1_w, w_in, conv_w, conv_b, gla_wa2, gla_ba, gla_onorm_w, q_norm_w, k_norm_w,
           sinks, w_branch, w_o, norm2_w, w_up, w_down):
    bsz, seq, d = x.shape
    depth = w_in.shape[0]
    assert d == D_MODEL and w_in.shape[2] == N_IN
    tok = min(TOK_TILE, seq)
    assert seq % tok == 0 and tok % SUB_TILE == 0 and SUB_TILE % (2 * SWA_BLOCK) == 0
    ffn_tile = min(FFN_TILE, bsz * seq)
    assert (bsz * seq) % ffn_tile == 0
    cos, sin = _rope_tables(positions)
    mixer_w = _first_mixer_weights(w_in, w_branch, w_o)
    for layer in range(depth):
        p = dict(mixer_w, **_small_mixer_params(layer, norm1_w, conv_w, conv_b, gla_wa2, gla_ba, gla_onorm_w,
                                                q_norm_w, k_norm_w, sinks))
        x, w_up_b16, w_down_b16 = _mixer_call(x, cos, sin, p, w_up[layer], w_down[layer], tok)
        nxt = (w_in[layer + 1], w_branch[layer + 1], w_o[layer + 1]) if layer + 1 < depth else None
        x2, mixer_w = _ffn_call(x.reshape(bsz * seq, d), norm2_w[layer][None, :], w_up_b16, w_down_b16, ffn_tile, nxt)
        x = x2.reshape(bsz, seq, d)
    return x
```

```python
import functools

import jax
import jax.numpy as jnp
import numpy as np
from jax import lax
from jax.experimental import pallas as pl
from jax.experimental.pallas import tpu as pltpu

F32 = jnp.float32
BF16 = jnp.bfloat16

D_MODEL = 1024
CONV_DIM = 512
CONV_WIDTH = 3
GLA_HEADS = 4
GLA_DK = 64
GLA_DV = 128
GLA_RANK = 16
GLA_TAU = 16.0
GLA_CHUNK = 64
SWA_HEADS = 8
SWA_KV_HEADS = 2
SWA_GROUP = SWA_HEADS // SWA_KV_HEADS
SWA_HEAD_DIM = 64
SWA_BLOCK = 128
ROPE_THETA = 500000.0
ROPE_DIM = SWA_HEAD_DIM // 4
N_BRANCHES = 3
BRANCH_WIDTH = 512
D_FF = 4 * D_MODEL
NORM_EPS = 1e-6
LOG2_E = 1.4426950408889634

LANES = 128
SUBLANES = 8
V7X_VMEM_BYTES = 64 * 1024 * 1024

GA_PAD = LANES
IN_A = 3 * CONV_DIM + 2 * GLA_HEADS * GLA_DK + 2 * GLA_HEADS * GLA_DV
IN_B = SWA_HEADS * SWA_HEAD_DIM + 2 * SWA_KV_HEADS * SWA_HEAD_DIM + N_BRANCHES * D_MODEL
N_IN = IN_A + GLA_RANK + IN_B
A_CONV = 0
A_GQ = A_CONV + 3 * CONV_DIM
A_GK = A_GQ + GLA_HEADS * GLA_DK
A_GV = A_GK + GLA_HEADS * GLA_DK
A_GR = A_GV + GLA_HEADS * GLA_DV
B_SQ = 0
B_SK = B_SQ + SWA_HEADS * SWA_HEAD_DIM
B_SV = B_SK + SWA_KV_HEADS * SWA_HEAD_DIM
B_GATE = B_SV + SWA_KV_HEADS * SWA_HEAD_DIM

ROPE_TOKENS_PER_ROW = LANES // (ROPE_DIM // 2)
ROPE_TILE = 1024
SUB_TILE = 256
TOK_TILE = 512
FFN_TILE = 1024
FF_CHUNK = 1024


def _dot(a, b):
    return jnp.dot(a, b, preferred_element_type=F32)


def _dot_nt(a, b):
    return lax.dot_general(a, b, (((1,), (1,)), ((), ())), preferred_element_type=F32)


def _split_dot(m_bf16, v_f32):
    hi = v_f32.astype(BF16)
    lo = (v_f32 - hi.astype(F32)).astype(BF16)
    return _dot(m_bf16, hi) + _dot(m_bf16, lo)


def _split_dot_rhs(v_f32, m_bf16):
    hi = v_f32.astype(BF16)
    lo = (v_f32 - hi.astype(F32)).astype(BF16)
    return _dot(hi, m_bf16) + _dot(lo, m_bf16)


def _rms_scale(x):
    return x * lax.rsqrt(jnp.mean(x * x, axis=-1, keepdims=True) + NORM_EPS)


def _lane_group_mask(width, group, index):
    lane = lax.broadcasted_iota(jnp.int32, (1, width), 1)
    return jnp.where((lane // group) == index, 1.0, 0.0).astype(F32)


def _split3(v_f32):
    hi = v_f32.astype(BF16)
    r1 = v_f32 - hi.astype(F32)
    mid = r1.astype(BF16)
    lo = (r1 - mid.astype(F32)).astype(BF16)
    return hi, mid, lo


def _rope_table_kernel(pos_ref, invf_ref, spread_ref, cos_ref, sin_ref, *, tokens):
    rows = tokens // ROPE_TOKENS_PER_ROW
    ang = pos_ref[...].astype(F32) * invf_ref[...]
    dense = jnp.concatenate([jnp.cos(ang), jnp.sin(ang)], axis=1)
    by_tok = jnp.broadcast_to(dense[:, None, :], (rows, ROPE_TOKENS_PER_ROW, 2 * LANES)).reshape(tokens, 2 * LANES)
    tok_c = lax.broadcasted_iota(jnp.int32, (tokens, 2 * LANES), 0) % ROPE_TOKENS_PER_ROW
    slot_c = (lax.broadcasted_iota(jnp.int32, (tokens, 2 * LANES), 1) % LANES) // (ROPE_DIM // 2)
    own = jnp.where(tok_c == slot_c, by_tok, 0.0)
    out = sum(_dot(piece, spread_ref[...]) for piece in _split3(own))
    lane_in_head = lax.broadcasted_iota(jnp.int32, (1, LANES), 1) % SWA_HEAD_DIM
    cos_ref[...] = out[:, 0:LANES] + jnp.where(lane_in_head >= ROPE_DIM, 1.0, 0.0)
    sin_ref[...] = out[:, LANES:2 * LANES]


def _rope(x, cos, sin_signed, lower_half):
    half = ROPE_DIM // 2
    up = pltpu.roll(x, LANES - half, axis=1)
    down = pltpu.roll(x, half, axis=1)
    return x * cos + jnp.where(lower_half, up, down) * sin_signed


def _mixer_kernel(sinks_ref, x_ref, cos_ref, sin_ref, n1w_ref, wina_ref, winga_ref, winb_ref, convw_ref, convb_ref,
                  wa2_ref, ba_ref, onw_ref, qnw_ref, knw_ref, wbr_ref, wo_ref, wup_f32_ref, wdown_f32_ref,
                  out_ref, wup_ref, wdown_ref, conv_buf, gla_state, k_prev, v_prev, *, tok):
    t = pl.program_id(1)

    @pl.when(t == 0)
    def _():
        conv_buf[0:SUBLANES, :] = jnp.zeros((SUBLANES, CONV_DIM), F32)
        gla_state[...] = jnp.zeros_like(gla_state)
        k_prev[...] = jnp.zeros_like(k_prev)
        v_prev[...] = jnp.zeros_like(v_prev)

    weights = (n1w_ref, wina_ref, winga_ref, winb_ref, convw_ref, convb_ref, wa2_ref, ba_ref, onw_ref, qnw_ref,
               knw_ref, wbr_ref, wo_ref)
    state = (conv_buf, gla_state, k_prev, v_prev)
    sinks = [sinks_ref[h] * LOG2_E for h in range(SWA_HEADS)]
    for s in range(tok // SUB_TILE):
        rows = slice(s * SUB_TILE, (s + 1) * SUB_TILE)
        no_prev_bias = jnp.where(t == 0, -jnp.inf, 0.0) if s == 0 else None
        out_ref[0, rows, :] = _mixer_subtile(x_ref[0, rows, :], cos_ref[0, rows, :], sin_ref[0, rows, :],
                                             no_prev_bias, sinks, weights, state)
    wup_ref[...] = wup_f32_ref[...].astype(BF16)
    wdown_ref[...] = wdown_f32_ref[...].astype(BF16)


def _mixer_subtile(x, cos, sin_signed, no_prev_bias, sinks, weights, state):
    (n1w_ref, wina_ref, winga_ref, winb_ref, convw_ref, convb_ref, wa2_ref, ba_ref, onw_ref, qnw_ref,
     knw_ref, wbr_ref, wo_ref) = weights
    conv_buf, gla_state, k_prev, v_prev = state
    tok = SUB_TILE
    n_chunks = tok // GLA_CHUNK
    n_blocks = tok // SWA_BLOCK

    ub = (_rms_scale(x) * n1w_ref[...]).astype(BF16)

    gate_cols = 2 * LANES
    gate_pieces = []

    def gate_slices(count):
        for _ in range(count):
            c0 = B_GATE + len(gate_pieces) * gate_cols
            gate_pieces.append(jax.nn.sigmoid(_dot(ub, winb_ref[:, c0:c0 + gate_cols])))

    ga = _dot(ub, winga_ref[...])
    pg = _dot(ub, wina_ref[:, A_GQ:IN_A])
    gq = pg[:, 0:A_GK - A_GQ] * (GLA_DK ** -0.5)
    gk = pg[:, A_GK - A_GQ:A_GV - A_GQ]
    gv = pg[:, A_GV - A_GQ:A_GR - A_GQ]
    gr = pg[:, A_GR - A_GQ:IN_A - A_GQ]
    ps = _dot(ub, winb_ref[:, B_SQ:B_GATE])
    sq = ps[:, B_SQ:B_SK]
    sk = ps[:, B_SK:B_SV]
    sv = ps[:, B_SV:B_GATE]

    z = _dot(ga.astype(BF16), wa2_ref[...]) + ba_ref[...]
    glog = (jnp.minimum(z, 0.0) - jnp.log(1.0 + jnp.exp(-jnp.abs(z)))) * (1.0 / GLA_TAU)
    gate_slices(1)

    gi = lax.broadcasted_iota(jnp.int32, (2 * LANES, 2 * LANES), 0) // SWA_HEAD_DIM
    gj = lax.broadcasted_iota(jnp.int32, (2 * LANES, 2 * LANES), 1) // SWA_HEAD_DIM
    head_ones = jnp.where(gi == gj, 1.0, 0.0).astype(BF16)
    n_qcols = SWA_HEADS * SWA_HEAD_DIM // (2 * LANES)
    q_halves = [sq[:, j * 2 * LANES:(j + 1) * 2 * LANES] for j in range(n_qcols)]
    ssq = [_dot((qj * qj).astype(BF16), head_ones) for qj in q_halves]
    ssk = _dot((sk * sk).astype(BF16), head_ones[0:LANES, 0:LANES])

    cx = _dot(ub, wina_ref[:, A_CONV:A_CONV + CONV_DIM])
    cc = _dot(ub, wina_ref[:, A_CONV + 2 * CONV_DIM:A_CONV + 3 * CONV_DIM])
    gate_slices(2)
    cb = _dot(ub, wina_ref[:, A_CONV + CONV_DIM:A_CONV + 2 * CONV_DIM])
    uc = cc * cx
    conv_buf[SUBLANES:SUBLANES + tok, :] = uc
    u1 = conv_buf[SUBLANES - 1:SUBLANES - 1 + tok, :]
    u2 = conv_buf[SUBLANES - 2:SUBLANES - 2 + tok, :]
    yconv = convw_ref[0:1, :] * u2 + convw_ref[1:2, :] * u1 + convw_ref[2:3, :] * uc
    ya = (cb * (yconv + convb_ref[...])).astype(BF16)
    conv_buf[0:SUBLANES, :] = uc[tok - SUBLANES:tok, :]

    ri = lax.broadcasted_iota(jnp.int32, (tok, tok), 0)
    ci = lax.broadcasted_iota(jnp.int32, (tok, tok), 1)
    same_chunk = (ri // GLA_CHUNK) == (ci // GLA_CHUNK)
    causal_chunk = same_chunk & (ci <= ri)
    l_tri = jnp.where(causal_chunk, 1.0, 0.0).astype(BF16)
    b = _split_dot(l_tri, glog)
    b_last = jnp.concatenate(
        [jnp.broadcast_to(b[(c + 1) * GLA_CHUNK - 1:(c + 1) * GLA_CHUNK, :], (GLA_CHUNK, GLA_HEADS * GLA_DK))
         for c in range(n_chunks)], axis=0)
    gate_slices(2)
    q_in_b = (gq * jnp.exp(b)).astype(BF16)
    k_in = (gk * jnp.exp(-b)).astype(BF16)
    k_end_b = (gk * jnp.exp(b_last - b)).astype(BF16)
    decay = jnp.exp(b_last)
    head_masks = [_lane_group_mask(GLA_HEADS * GLA_DK, GLA_DK, h).astype(BF16) for h in range(GLA_HEADS)]
    gvb = gv.astype(BF16)

    lower_half = (lax.broadcasted_iota(jnp.int32, (1, LANES), 1) % SWA_HEAD_DIM) < ROPE_DIM // 2
    q_cols = []
    for j in range(n_qcols):
        qn = q_halves[j] * lax.rsqrt(ssq[j] * (1.0 / SWA_HEAD_DIM) + NORM_EPS)
        for i in range(2):
            qn_i = qn[:, i * LANES:(i + 1) * LANES] * qnw_ref[...]
            q_cols.append((_rope(qn_i, cos, sin_signed, lower_half) * (SWA_HEAD_DIM ** -0.5 * LOG2_E)).astype(BF16))
    kn = sk * lax.rsqrt(ssk * (1.0 / SWA_HEAD_DIM) + NORM_EPS) * knw_ref[...]
    kr = _rope(kn, cos, sin_signed, lower_half)

    upd = []
    for c in range(n_chunks):
        r0 = c * GLA_CHUNK
        ke_c = k_end_b[r0:r0 + GLA_CHUNK, :]
        k_stack = jnp.concatenate([ke_c * head_masks[h] for h in range(GLA_HEADS)], axis=0)
        v_stack = jnp.concatenate(
            [gv[r0:r0 + GLA_CHUNK, h * GLA_DV:(h + 1) * GLA_DV] for h in range(GLA_HEADS)], axis=0)
        upd.append(_dot(v_stack.T.astype(BF16), k_stack))

    group_masks = [_lane_group_mask(2 * LANES, SWA_HEAD_DIM, g).astype(BF16) for g in range(SWA_GROUP)]
    lane128 = lax.broadcasted_iota(jnp.int32, (1, LANES), 1)
    slot_j = lax.broadcasted_iota(jnp.int32, (SWA_BLOCK, SWA_GROUP * SWA_BLOCK), 0)
    qry_i = lax.broadcasted_iota(jnp.int32, (SWA_BLOCK, SWA_GROUP * SWA_BLOCK), 1) % SWA_BLOCK
    from_prev = slot_j > qry_i

    def kv_keys(k_blk):
        k_sw = pltpu.roll(k_blk, SWA_HEAD_DIM, axis=1)
        reps = []
        for kv in range(SWA_KV_HEADS):
            k_one = jnp.where((lane128 // SWA_HEAD_DIM) == kv, k_blk, k_sw).astype(BF16)
            reps.append(jnp.concatenate([k_one, k_one], axis=1))
        return reps

    keys = [kv_keys(k_prev[...])] + [kv_keys(kr[n * SWA_BLOCK:(n + 1) * SWA_BLOCK, :]) for n in range(n_blocks)]
    vt_f32 = [sv[n * SWA_BLOCK:(n + 1) * SWA_BLOCK, :].T for n in range(n_blocks)]
    vt = [v_prev[...].astype(BF16)] + [v.astype(BF16) for v in vt_f32]

    def swa_scores(n, kv):
        r0 = n * SWA_BLOCK
        q_kv = jnp.concatenate(q_cols[2 * kv:2 * kv + 2], axis=1)[r0:r0 + SWA_BLOCK, :]
        q_stack = jnp.concatenate([q_kv * group_masks[g] for g in range(SWA_GROUP)], axis=0)
        k_both = jnp.concatenate([keys[n][kv], keys[n + 1][kv]], axis=0)
        return _dot_nt(k_both, q_stack)

    def swa_softmax(n, kv, s_both):
        s_prev = s_both[0:SWA_BLOCK, :]
        if n == 0 and no_prev_bias is not None:
            s_prev = s_prev + no_prev_bias
        s_t = jnp.where(from_prev, s_prev, s_both[SWA_BLOCK:2 * SWA_BLOCK, :])
        sink = jnp.concatenate(
            [jnp.full((1, SWA_BLOCK), sinks[kv * SWA_GROUP + g], F32) for g in range(SWA_GROUP)], axis=1)
        m = jnp.maximum(jnp.max(s_t, axis=0, keepdims=True), sink)
        p_t = jnp.exp2(s_t - m)
        inv = 1.0 / (jnp.sum(p_t, axis=0, keepdims=True) + jnp.exp2(sink - m))
        p_both = jnp.concatenate([jnp.where(from_prev, p_t, 0.0), jnp.where(from_prev, 0.0, p_t)], axis=0)
        return p_both.astype(BF16), inv

    def swa_values(n, kv, p_both, inv):
        hd = slice(kv * SWA_HEAD_DIM, (kv + 1) * SWA_HEAD_DIM)
        vt_both = jnp.concatenate([vt[n][hd, :], vt[n + 1][hd, :]], axis=1)
        o_t = _dot(vt_both, p_both) * inv
        return [o_t[:, g * SWA_BLOCK:(g + 1) * SWA_BLOCK] for g in range(SWA_GROUP)]

    def gla_scores(h):
        att = _dot_nt(q_in_b * head_masks[h], k_in)
        return jnp.where(causal_chunk, att, 0.0).astype(BF16)

    def gla_values(h, att):
        return _dot(att, gvb[:, h * GLA_DV:(h + 1) * GLA_DV])

    steps = [(n, kv) for n in range(n_blocks) for kv in range(SWA_KV_HEADS)]
    assert len(steps) == 4 and GLA_HEADS == 4 and n_chunks == 4

    s0 = swa_scores(*steps[0])
    att0 = gla_scores(0)
    att1 = gla_scores(1)

    state = gla_state[...]
    o_inter = [[] for _ in range(GLA_HEADS)]
    for c in range(n_chunks):
        r0 = c * GLA_CHUNK
        q_c = q_in_b[r0:r0 + GLA_CHUNK, :]
        q_stack = jnp.concatenate([q_c * head_masks[h] for h in range(GLA_HEADS)], axis=0)
        oi = _dot_nt(q_stack, state.astype(BF16))
        for h in range(GLA_HEADS):
            o_inter[h].append(oi[h * GLA_CHUNK:(h + 1) * GLA_CHUNK, :])
        state = decay[r0:r0 + 1, :] * state + upd[c]
    gla_state[...] = state

    s1 = swa_scores(*steps[1])
    o_heads = swa_values(*steps[0], *swa_softmax(*steps[0], s0))
    ov0 = gla_values(0, att0)
    ov1 = gla_values(1, att1)
    att2 = gla_scores(2)
    att3 = gla_scores(3)
    yc_blocks = []
    s2 = swa_scores(*steps[2])
    o_heads += swa_values(*steps[1], *swa_softmax(*steps[1], s1))
    yc_blocks.append(jnp.concatenate(o_heads, axis=0).T)
    gate_slices(1)
    ov2 = gla_values(2, att2)
    ov3 = gla_values(3, att3)
    s3 = swa_scores(*steps[3])
    o_heads = swa_values(*steps[2], *swa_softmax(*steps[2], s2))
    gate_slices(2)
    o_heads += swa_values(*steps[3], *swa_softmax(*steps[3], s3))
    yc_blocks.append(jnp.concatenate(o_heads, axis=0).T)
    yc = jnp.concatenate(yc_blocks, axis=0).astype(BF16)
    k_prev[...] = kr[tok - SWA_BLOCK:tok, :]
    v_prev[...] = vt_f32[-1]

    br_a = _dot(ya, wbr_ref[0])
    gate_slices(3)

    yb_parts = []
    for h, ov in enumerate((ov0, ov1, ov2, ov3)):
        o_h = ov + jnp.concatenate(o_inter[h], axis=0)
        o_h = _rms_scale(o_h) * onw_ref[...]
        r_h = gr[:, h * GLA_DV:(h + 1) * GLA_DV]
        yb_parts.append((o_h * (r_h * jax.nn.sigmoid(r_h))).astype(BF16))
    yb = jnp.concatenate(yb_parts, axis=1)
    br_b = _dot(yb, wbr_ref[1])
    gate_slices(1)

    per_branch = D_MODEL // gate_cols
    assert len(gate_pieces) == N_BRANCHES * per_branch
    acc = x
    mixed_slices = []
    for j in range(per_branch):
        cols = slice(j * gate_cols, (j + 1) * gate_cols)
        br_c_j = _dot(yc, wbr_ref[2, :, cols])
        mixed_j = (gate_pieces[j] * br_a[:, cols] + gate_pieces[per_branch + j] * br_b[:, cols]
                   + gate_pieces[2 * per_branch + j] * br_c_j)
        mixed_slices.append(mixed_j.astype(BF16))
        if j >= 1:
            acc = acc + _dot(mixed_slices[j - 1], wo_ref[(j - 1) * gate_cols:j * gate_cols, :])
    return acc + _dot(mixed_slices[-1], wo_ref[(per_branch - 1) * gate_cols:per_branch * gate_cols, :])


def _ffn_kernel(x_ref, n2w_ref, wup_ref, wdown_ref, *rest):
    if len(rest) == 1:
        (out_ref,) = rest
    else:
        win_f32_ref, wbr_f32_ref, wo_f32_ref, out_ref, wina_ref, winga_ref, winb_ref, wbr_ref, wo_ref = rest
        rows = win_f32_ref.shape[0]
        wina_ref[...] = win_f32_ref[:, 0:IN_A].astype(BF16)
        lane = lax.broadcasted_iota(jnp.int32, (rows, GA_PAD), 1)
        winga_ref[...] = jnp.where(lane < GLA_RANK, win_f32_ref[:, IN_A:IN_A + GA_PAD], 0.0).astype(BF16)
        winb_ref[...] = win_f32_ref[:, IN_A + GLA_RANK:N_IN].astype(BF16)
        wbr_ref[...] = wbr_f32_ref[...].astype(BF16)
        wo_ref[...] = wo_f32_ref[...].astype(BF16)
    x = x_ref[...]
    hb = (_rms_scale(x) * n2w_ref[...]).astype(BF16)
    acc = x
    for c in range(D_FF // FF_CHUNK):
        a = jnp.maximum(_dot(hb, wup_ref[:, c * FF_CHUNK:(c + 1) * FF_CHUNK]), 0.0)
        acc = acc + _dot((a * a).astype(BF16), wdown_ref[c * FF_CHUNK:(c + 1) * FF_CHUNK, :])
    out_ref[...] = acc


def _const_spec(shape):
    zeros = (0,) * len(shape)
    return pl.BlockSpec(shape, lambda *_: zeros, pipeline_mode=pl.Buffered(1))


def _row_block(n_rows, n_steps, dtype):
    rows = n_rows // n_steps
    assert rows * n_steps == n_rows and rows % (2 * SUBLANES if dtype == BF16 else SUBLANES) == 0
    return rows


def _mixer_call(x, cos, sin, p, w_up_f32, w_down_f32, layer, tok):
    bsz, seq, d = x.shape
    n_t = seq // tok
    grid = (bsz, n_t)
    n_steps = bsz * n_t
    tile = lambda b, t: (b, t, 0)
    step_rows = lambda b, t: (b * n_t + t, 0)
    layer_rows = lambda b, t: (layer, b * n_t + t, 0)
    up_rows = _row_block(d, n_steps, BF16)
    down_rows = _row_block(D_FF, n_steps, BF16)
    spec = _const_spec
    in_specs = [
        pl.BlockSpec(memory_space=pltpu.SMEM),
        pl.BlockSpec((1, tok, d), tile),
        pl.BlockSpec((1, tok, LANES), tile),
        pl.BlockSpec((1, tok, LANES), tile),
        spec((1, d)),
        spec((d, IN_A)),
        spec((d, GA_PAD)),
        spec((d, IN_B)),
        spec((CONV_WIDTH, CONV_DIM)),
        spec((1, CONV_DIM)),
        spec((GA_PAD, GLA_HEADS * GLA_DK)),
        spec((1, GLA_HEADS * GLA_DK)),
        spec((1, GLA_DV)),
        spec((1, LANES)),
        spec((1, LANES)),
        spec((N_BRANCHES, BRANCH_WIDTH, d)),
        spec((d, d)),
        pl.BlockSpec((None, up_rows, D_FF), layer_rows),
        pl.BlockSpec((None, down_rows, d), layer_rows),
    ]
    out_specs = [
        pl.BlockSpec((1, tok, d), tile),
        pl.BlockSpec((up_rows, D_FF), step_rows),
        pl.BlockSpec((down_rows, d), step_rows),
    ]
    out_shape = [
        jax.ShapeDtypeStruct(x.shape, F32),
        jax.ShapeDtypeStruct((d, D_FF), BF16),
        jax.ShapeDtypeStruct((D_FF, d), BF16),
    ]
    scratch = [
        pltpu.VMEM((SUB_TILE + SUBLANES, CONV_DIM), F32),
        pltpu.VMEM((GLA_DV, GLA_HEADS * GLA_DK), F32),
        pltpu.VMEM((SWA_BLOCK, SWA_KV_HEADS * SWA_HEAD_DIM), F32),
        pltpu.VMEM((SWA_BLOCK, SWA_KV_HEADS * SWA_HEAD_DIM), F32),
    ]
    out, w_up_b16, w_down_b16 = pl.pallas_call(
        functools.partial(_mixer_kernel, tok=tok),
        grid=grid,
        in_specs=in_specs,
        out_specs=out_specs,
        out_shape=out_shape,
        scratch_shapes=scratch,
        compiler_params=pltpu.CompilerParams(
            dimension_semantics=("arbitrary", "arbitrary"),
            vmem_limit_bytes=V7X_VMEM_BYTES * 3 // 4),
        name="mixer",
    )(p["sinks"], x, cos, sin, p["n1w"], p["w_in_a"], p["w_in_ga"], p["w_in_b"], p["conv_w"], p["conv_b"],
      p["wa2"], p["ba"], p["onw"], p["qnw"], p["knw"], p["w_branch"], p["w_o"], w_up_f32, w_down_f32)
    return out, w_up_b16, w_down_b16


def _ffn_call(x2, n2w, w_up, w_down, tile, next_mixer_f32=None, next_layer=None):
    n, d = x2.shape
    n_steps = n // tile
    step_rows = lambda i: (i, 0)
    in_specs = [
        pl.BlockSpec((tile, d), step_rows),
        _const_spec((1, d)),
        _const_spec((d, D_FF)),
        _const_spec((D_FF, d)),
    ]
    out_specs = [pl.BlockSpec((tile, d), step_rows)]
    out_shape = [jax.ShapeDtypeStruct(x2.shape, F32)]
    operands = [x2, n2w, w_up, w_down]
    if next_mixer_f32 is not None:
        w_in_f32, w_branch_f32, w_o_f32 = next_mixer_f32
        w_branch_f32 = w_branch_f32.reshape(-1, N_BRANCHES * BRANCH_WIDTH, d)
        layer_rows = lambda i: (next_layer, i, 0)
        in_rows = _row_block(d, n_steps, BF16)
        br_rows = _row_block(N_BRANCHES * BRANCH_WIDTH, n_steps, BF16)
        in_specs += [
            pl.BlockSpec((None, in_rows, N_IN), layer_rows),
            pl.BlockSpec((None, br_rows, d), layer_rows),
            pl.BlockSpec((None, in_rows, d), layer_rows),
        ]
        out_specs += [
            pl.BlockSpec((in_rows, IN_A), step_rows),
            pl.BlockSpec((in_rows, GA_PAD), step_rows),
            pl.BlockSpec((in_rows, IN_B), step_rows),
            pl.BlockSpec((br_rows, d), step_rows),
            pl.BlockSpec((in_rows, d), step_rows),
        ]
        out_shape += [
            jax.ShapeDtypeStruct((d, IN_A), BF16),
            jax.ShapeDtypeStruct((d, GA_PAD), BF16),
            jax.ShapeDtypeStruct((d, IN_B), BF16),
            jax.ShapeDtypeStruct((N_BRANCHES * BRANCH_WIDTH, d), BF16),
            jax.ShapeDtypeStruct((d, d), BF16),
        ]
        operands += [w_in_f32, w_branch_f32, w_o_f32]
    outs = pl.pallas_call(
        _ffn_kernel,
        grid=(n_steps,),
        in_specs=in_specs,
        out_specs=out_specs,
        out_shape=out_shape,
        compiler_params=pltpu.CompilerParams(
            dimension_semantics=("arbitrary",),
            vmem_limit_bytes=V7X_VMEM_BYTES * 3 // 4),
        name="ffn",
    )(*operands)
    if next_mixer_f32 is None:
        return outs[0], None
    y, w_in_a, w_in_ga, w_in_b, w_branch, w_o = outs
    return y, {"w_in_a": w_in_a, "w_in_ga": w_in_ga, "w_in_b": w_in_b,
               "w_branch": w_branch.reshape(N_BRANCHES, BRANCH_WIDTH, d), "w_o": w_o}


def _rope_spread_matrix():
    half = ROPE_DIM // 2
    c = np.arange(LANES)[:, None]
    l = np.arange(LANES)[None, :]
    hit = ((c % half) == (l % half)) & ((l % SWA_HEAD_DIM) < ROPE_DIM)
    sign = np.where((l % SWA_HEAD_DIM) < half, -1.0, 1.0)
    m = np.zeros((2 * LANES, 2 * LANES), np.float32)
    m[:LANES, :LANES] = hit
    m[LANES:, LANES:] = hit * sign
    return jnp.asarray(m, BF16)


def _rope_tables(positions):
    bsz, seq = positions.shape
    n = bsz * seq
    tokens = min(ROPE_TILE, n)
    assert n % tokens == 0 and tokens % (ROPE_TOKENS_PER_ROW * SUBLANES) == 0
    half = ROPE_DIM // 2
    inv_freq = ROPE_THETA ** (-jnp.arange(0, ROPE_DIM, 2, dtype=F32) / ROPE_DIM)
    invf_dense = jnp.tile(inv_freq, LANES // half)[None, :]
    pos_dense = jnp.repeat(positions.astype(jnp.int32).reshape(n), half).reshape(n // ROPE_TOKENS_PER_ROW, LANES)
    rows = tokens // ROPE_TOKENS_PER_ROW
    cos, sin = pl.pallas_call(
        functools.partial(_rope_table_kernel, tokens=tokens),
        grid=(n // tokens,),
        in_specs=[
            pl.BlockSpec((rows, LANES), lambda i: (i, 0)),
            _const_spec((1, LANES)),
            _const_spec((2 * LANES, 2 * LANES)),
        ],
        out_specs=[pl.BlockSpec((tokens, LANES), lambda i: (i, 0))] * 2,
        out_shape=[jax.ShapeDtypeStruct((n, LANES), F32)] * 2,
        compiler_params=pltpu.CompilerParams(dimension_semantics=("arbitrary",)),
        name="rope_tables",
    )(pos_dense, invf_dense, _rope_spread_matrix())
    return cos.reshape(bsz, seq, LANES), sin.reshape(bsz, seq, LANES)


def _first_mixer_weights(w_in, w_branch, w_o):
    w = w_in[0].astype(BF16)
    return {
        "w_in_a": w[:, :IN_A],
        "w_in_ga": jnp.pad(w[:, IN_A:IN_A + GLA_RANK], ((0, 0), (0, GA_PAD - GLA_RANK))),
        "w_in_b": w[:, IN_A + GLA_RANK:],
        "w_branch": w_branch[0].astype(BF16),
        "w_o": w_o[0].astype(BF16),
    }


def _small_mixer_params(layer, norm1_w, conv_w, conv_b, gla_wa2, gla_ba, gla_onorm_w, q_norm_w, k_norm_w, sinks):
    heads_per_tile = LANES // SWA_HEAD_DIM
    return {
        "sinks": sinks[layer].astype(F32),
        "n1w": norm1_w[layer][None, :],
        "conv_w": conv_w[layer],
        "conv_b": conv_b[layer][None, :],
        "wa2": jnp.pad(gla_wa2[layer], ((0, GA_PAD - GLA_RANK), (0, 0))).astype(BF16),
        "ba": gla_ba[layer][None, :],
        "onw": gla_onorm_w[layer][None, :],
        "qnw": jnp.tile(q_norm_w[layer], heads_per_tile)[None, :],
        "knw": jnp.tile(k_norm_w[layer], heads_per_tile)[None, :],
    }


def kernel(x, positions, norm1_w, w_in, conv_w, conv_b, gla_wa2, gla_ba, gla_onorm_w, q_norm_w, k_norm_w,
           sinks, w_branch, w_o, norm2_w, w_up, w_down):
    bsz, seq, d = x.shape
    depth = w_in.shape[0]
    assert d == D_MODEL and w_in.shape[2] == N_IN
    tok = min(TOK_TILE, seq)
    assert seq % tok == 0 and tok % SUB_TILE == 0 and SUB_TILE % (2 * SWA_BLOCK) == 0
    ffn_tile = min(FFN_TILE, bsz * seq)
    assert (bsz * seq) % ffn_tile == 0
    cos, sin = _rope_tables(positions)
    mixer_w = _first_mixer_weights(w_in, w_branch, w_o)
    for layer in range(depth):
        p = dict(mixer_w, **_small_mixer_params(layer, norm1_w, conv_w, conv_b, gla_wa2, gla_ba, gla_onorm_w,
                                                q_norm_w, k_norm_w, sinks))
        x, w_up_b16, w_down_b16 = _mixer_call(x, cos, sin, p, w_up, w_down, layer, tok)
        nxt = (w_in, w_branch, w_o) if layer + 1 < depth else None
        x2, mixer_w = _ffn_call(x.reshape(bsz * seq, d), norm2_w[layer][None, :], w_up_b16, w_down_b16, ffn_tile,
                                nxt, layer + 1)
        x = x2.reshape(bsz, seq, d)
    return x
```

```python
import functools

import jax
import jax.numpy as jnp
import numpy as np
from jax import lax
from jax.experimental import pallas as pl
from jax.experimental.pallas import tpu as pltpu

F32 = jnp.float32
BF16 = jnp.bfloat16

D_MODEL = 1024
CONV_DIM = 512
CONV_WIDTH = 3
GLA_HEADS = 4
GLA_DK = 64
GLA_DV = 128
GLA_RANK = 16
GLA_TAU = 16.0
GLA_CHUNK = 64
SWA_HEADS = 8
SWA_KV_HEADS = 2
SWA_GROUP = SWA_HEADS // SWA_KV_HEADS
SWA_HEAD_DIM = 64
SWA_BLOCK = 128
ROPE_THETA = 500000.0
ROPE_DIM = SWA_HEAD_DIM // 4
N_BRANCHES = 3
BRANCH_WIDTH = 512
D_FF = 4 * D_MODEL
NORM_EPS = 1e-6
LOG2_E = 1.4426950408889634

LANES = 128
SUBLANES = 8
V7X_VMEM_BYTES = 64 * 1024 * 1024

GA_PAD = LANES
IN_A = 3 * CONV_DIM + 2 * GLA_HEADS * GLA_DK + 2 * GLA_HEADS * GLA_DV
IN_B = SWA_HEADS * SWA_HEAD_DIM + 2 * SWA_KV_HEADS * SWA_HEAD_DIM + N_BRANCHES * D_MODEL
N_IN = IN_A + GLA_RANK + IN_B
A_CONV = 0
A_GQ = A_CONV + 3 * CONV_DIM
A_GK = A_GQ + GLA_HEADS * GLA_DK
A_GV = A_GK + GLA_HEADS * GLA_DK
A_GR = A_GV + GLA_HEADS * GLA_DV
B_SQ = 0
B_SK = B_SQ + SWA_HEADS * SWA_HEAD_DIM
B_SV = B_SK + SWA_KV_HEADS * SWA_HEAD_DIM
B_GATE = B_SV + SWA_KV_HEADS * SWA_HEAD_DIM

ROPE_TOKENS_PER_ROW = LANES // (ROPE_DIM // 2)
ROPE_TILE = 1024
SUB_TILE = 256
TOK_TILE = 512
FFN_TILE = 1024
FF_CHUNK = 1024
CAST_ROWS = 128


def _dot(a, b):
    return jnp.dot(a, b, preferred_element_type=F32)


def _dot_nt(a, b):
    return lax.dot_general(a, b, (((1,), (1,)), ((), ())), preferred_element_type=F32)


def _split_dot(m_bf16, v_f32):
    hi = v_f32.astype(BF16)
    lo = (v_f32 - hi.astype(F32)).astype(BF16)
    return _dot(m_bf16, hi) + _dot(m_bf16, lo)


def _split_dot_rhs(v_f32, m_bf16):
    hi = v_f32.astype(BF16)
    lo = (v_f32 - hi.astype(F32)).astype(BF16)
    return _dot(hi, m_bf16) + _dot(lo, m_bf16)


def _rms_scale(x):
    return x * lax.rsqrt(jnp.mean(x * x, axis=-1, keepdims=True) + NORM_EPS)


def _lane_group_mask(width, group, index):
    lane = lax.broadcasted_iota(jnp.int32, (1, width), 1)
    return jnp.where((lane // group) == index, 1.0, 0.0).astype(F32)


def _split3(v_f32):
    hi = v_f32.astype(BF16)
    r1 = v_f32 - hi.astype(F32)
    mid = r1.astype(BF16)
    lo = (r1 - mid.astype(F32)).astype(BF16)
    return hi, mid, lo


def _rope_table_kernel(pos_ref, invf_ref, spread_ref, cos_ref, sin_ref, *, tokens):
    rows = tokens // ROPE_TOKENS_PER_ROW
    ang = pos_ref[...].astype(F32) * invf_ref[...]
    dense = jnp.concatenate([jnp.cos(ang), jnp.sin(ang)], axis=1)
    by_tok = jnp.broadcast_to(dense[:, None, :], (rows, ROPE_TOKENS_PER_ROW, 2 * LANES)).reshape(tokens, 2 * LANES)
    tok_c = lax.broadcasted_iota(jnp.int32, (tokens, 2 * LANES), 0) % ROPE_TOKENS_PER_ROW
    slot_c = (lax.broadcasted_iota(jnp.int32, (tokens, 2 * LANES), 1) % LANES) // (ROPE_DIM // 2)
    own = jnp.where(tok_c == slot_c, by_tok, 0.0)
    out = sum(_dot(piece, spread_ref[...]) for piece in _split3(own))
    lane_in_head = lax.broadcasted_iota(jnp.int32, (1, LANES), 1) % SWA_HEAD_DIM
    cos_ref[...] = out[:, 0:LANES] + jnp.where(lane_in_head >= ROPE_DIM, 1.0, 0.0)
    sin_ref[...] = out[:, LANES:2 * LANES]


def _rope(x, cos, sin_signed, lower_half):
    half = ROPE_DIM // 2
    up = pltpu.roll(x, LANES - half, axis=1)
    down = pltpu.roll(x, half, axis=1)
    return x * cos + jnp.where(lower_half, up, down) * sin_signed


def _mixer_kernel(sinks_ref, x_ref, cos_ref, sin_ref, n1w_ref, wina_ref, winga_ref, winb_ref, convw_ref, convb_ref,
                  wa2_ref, ba_ref, onw_ref, qnw_ref, knw_ref, wbr_ref, wo_ref, wup_f32_ref, wdown_f32_ref,
                  out_ref, wup_ref, wdown_ref, conv_buf, gla_state, k_prev, v_prev, *, tok):
    t = pl.program_id(1)

    @pl.when(t == 0)
    def _():
        conv_buf[0:SUBLANES, :] = jnp.zeros((SUBLANES, CONV_DIM), F32)
        gla_state[...] = jnp.zeros_like(gla_state)
        k_prev[...] = jnp.zeros_like(k_prev)
        v_prev[...] = jnp.zeros_like(v_prev)

    weights = (n1w_ref, wina_ref, winga_ref, winb_ref, convw_ref, convb_ref, wa2_ref, ba_ref, onw_ref, qnw_ref,
               knw_ref, wbr_ref, wo_ref)
    state = (conv_buf, gla_state, k_prev, v_prev)
    sinks = [sinks_ref[h] * LOG2_E for h in range(SWA_HEADS)]
    for s in range(tok // SUB_TILE):
        rows = slice(s * SUB_TILE, (s + 1) * SUB_TILE)
        no_prev_bias = jnp.where(t == 0, -jnp.inf, 0.0) if s == 0 else None
        out_ref[0, rows, :] = _mixer_subtile(x_ref[0, rows, :], cos_ref[0, rows, :], sin_ref[0, rows, :],
                                             no_prev_bias, sinks, weights, state)
    wup_ref[...] = wup_f32_ref[...].astype(BF16)
    wdown_ref[...] = wdown_f32_ref[...].astype(BF16)


def _mixer_subtile(x, cos, sin_signed, no_prev_bias, sinks, weights, state):
    (n1w_ref, wina_ref, winga_ref, winb_ref, convw_ref, convb_ref, wa2_ref, ba_ref, onw_ref, qnw_ref,
     knw_ref, wbr_ref, wo_ref) = weights
    conv_buf, gla_state, k_prev, v_prev = state
    tok = SUB_TILE
    n_chunks = tok // GLA_CHUNK
    n_blocks = tok // SWA_BLOCK

    ub = (_rms_scale(x) * n1w_ref[...]).astype(BF16)

    gate_cols = 2 * LANES
    gate_pieces = []

    def gate_slices(count):
        for _ in range(count):
            c0 = B_GATE + len(gate_pieces) * gate_cols
            gate_pieces.append(jax.nn.sigmoid(_dot(ub, winb_ref[:, c0:c0 + gate_cols])))

    ga = _dot(ub, winga_ref[...])
    pg = _dot(ub, wina_ref[:, A_GQ:IN_A])
    gq = pg[:, 0:A_GK - A_GQ] * (GLA_DK ** -0.5)
    gk = pg[:, A_GK - A_GQ:A_GV - A_GQ]
    gv = pg[:, A_GV - A_GQ:A_GR - A_GQ]
    gr = pg[:, A_GR - A_GQ:IN_A - A_GQ]
    ps = _dot(ub, winb_ref[:, B_SQ:B_GATE])
    sq = ps[:, B_SQ:B_SK]
    sk = ps[:, B_SK:B_SV]
    sv = ps[:, B_SV:B_GATE]

    z = _dot(ga.astype(BF16), wa2_ref[...]) + ba_ref[...]
    glog = (jnp.minimum(z, 0.0) - jnp.log(1.0 + jnp.exp(-jnp.abs(z)))) * (1.0 / GLA_TAU)
    gate_slices(1)

    gi = lax.broadcasted_iota(jnp.int32, (2 * LANES, 2 * LANES), 0) // SWA_HEAD_DIM
    gj = lax.broadcasted_iota(jnp.int32, (2 * LANES, 2 * LANES), 1) // SWA_HEAD_DIM
    head_ones = jnp.where(gi == gj, 1.0, 0.0).astype(BF16)
    n_qcols = SWA_HEADS * SWA_HEAD_DIM // (2 * LANES)
    q_halves = [sq[:, j * 2 * LANES:(j + 1) * 2 * LANES] for j in range(n_qcols)]
    ssq = [_dot((qj * qj).astype(BF16), head_ones) for qj in q_halves]
    ssk = _dot((sk * sk).astype(BF16), head_ones[0:LANES, 0:LANES])

    cx = _dot(ub, wina_ref[:, A_CONV:A_CONV + CONV_DIM])
    cc = _dot(ub, wina_ref[:, A_CONV + 2 * CONV_DIM:A_CONV + 3 * CONV_DIM])
    gate_slices(2)
    cb = _dot(ub, wina_ref[:, A_CONV + CONV_DIM:A_CONV + 2 * CONV_DIM])
    uc = cc * cx
    conv_buf[SUBLANES:SUBLANES + tok, :] = uc
    u1 = conv_buf[SUBLANES - 1:SUBLANES - 1 + tok, :]
    u2 = conv_buf[SUBLANES - 2:SUBLANES - 2 + tok, :]
    yconv = convw_ref[0:1, :] * u2 + convw_ref[1:2, :] * u1 + convw_ref[2:3, :] * uc
    ya = (cb * (yconv + convb_ref[...])).astype(BF16)
    conv_buf[0:SUBLANES, :] = uc[tok - SUBLANES:tok, :]

    ri = lax.broadcasted_iota(jnp.int32, (tok, tok), 0)
    ci = lax.broadcasted_iota(jnp.int32, (tok, tok), 1)
    same_chunk = (ri // GLA_CHUNK) == (ci // GLA_CHUNK)
    causal_chunk = same_chunk & (ci <= ri)
    l_tri = jnp.where(causal_chunk, 1.0, 0.0).astype(BF16)
    b = _split_dot(l_tri, glog)
    b_last = jnp.concatenate(
        [jnp.broadcast_to(b[(c + 1) * GLA_CHUNK - 1:(c + 1) * GLA_CHUNK, :], (GLA_CHUNK, GLA_HEADS * GLA_DK))
         for c in range(n_chunks)], axis=0)
    gate_slices(2)
    q_in_b = (gq * jnp.exp(b)).astype(BF16)
    k_in = (gk * jnp.exp(-b)).astype(BF16)
    k_end_b = (gk * jnp.exp(b_last - b)).astype(BF16)
    decay = jnp.exp(b_last)
    head_masks = [_lane_group_mask(GLA_HEADS * GLA_DK, GLA_DK, h).astype(BF16) for h in range(GLA_HEADS)]
    gvb = gv.astype(BF16)

    lower_half = (lax.broadcasted_iota(jnp.int32, (1, LANES), 1) % SWA_HEAD_DIM) < ROPE_DIM // 2
    q_cols = []
    for j in range(n_qcols):
        qn = q_halves[j] * lax.rsqrt(ssq[j] * (1.0 / SWA_HEAD_DIM) + NORM_EPS)
        for i in range(2):
            qn_i = qn[:, i * LANES:(i + 1) * LANES] * qnw_ref[...]
            q_cols.append((_rope(qn_i, cos, sin_signed, lower_half) * (SWA_HEAD_DIM ** -0.5 * LOG2_E)).astype(BF16))
    kn = sk * lax.rsqrt(ssk * (1.0 / SWA_HEAD_DIM) + NORM_EPS) * knw_ref[...]
    kr = _rope(kn, cos, sin_signed, lower_half)

    upd = []
    for c in range(n_chunks):
        r0 = c * GLA_CHUNK
        ke_c = k_end_b[r0:r0 + GLA_CHUNK, :]
        k_stack = jnp.concatenate([ke_c * head_masks[h] for h in range(GLA_HEADS)], axis=0)
        v_stack = jnp.concatenate(
            [gv[r0:r0 + GLA_CHUNK, h * GLA_DV:(h + 1) * GLA_DV] for h in range(GLA_HEADS)], axis=0)
        upd.append(_dot(v_stack.T.astype(BF16), k_stack))

    group_masks = [_lane_group_mask(2 * LANES, SWA_HEAD_DIM, g).astype(BF16) for g in range(SWA_GROUP)]
    lane128 = lax.broadcasted_iota(jnp.int32, (1, LANES), 1)
    slot_j = lax.broadcasted_iota(jnp.int32, (SWA_BLOCK, SWA_GROUP * SWA_BLOCK), 0)
    qry_i = lax.broadcasted_iota(jnp.int32, (SWA_BLOCK, SWA_GROUP * SWA_BLOCK), 1) % SWA_BLOCK
    from_prev = slot_j > qry_i

    def kv_keys(k_blk):
        k_sw = pltpu.roll(k_blk, SWA_HEAD_DIM, axis=1)
        reps = []
        for kv in range(SWA_KV_HEADS):
            k_one = jnp.where((lane128 // SWA_HEAD_DIM) == kv, k_blk, k_sw).astype(BF16)
            reps.append(jnp.concatenate([k_one, k_one], axis=1))
        return reps

    keys = [kv_keys(k_prev[...])] + [kv_keys(kr[n * SWA_BLOCK:(n + 1) * SWA_BLOCK, :]) for n in range(n_blocks)]
    vt_f32 = [sv[n * SWA_BLOCK:(n + 1) * SWA_BLOCK, :].T for n in range(n_blocks)]
    vt = [v_prev[...].astype(BF16)] + [v.astype(BF16) for v in vt_f32]

    def swa_scores(n, kv):
        r0 = n * SWA_BLOCK
        q_kv = jnp.concatenate(q_cols[2 * kv:2 * kv + 2], axis=1)[r0:r0 + SWA_BLOCK, :]
        q_stack = jnp.concatenate([q_kv * group_masks[g] for g in range(SWA_GROUP)], axis=0)
        k_both = jnp.concatenate([keys[n][kv], keys[n + 1][kv]], axis=0)
        return _dot_nt(k_both, q_stack)

    def swa_softmax(n, kv, s_both):
        s_prev = s_both[0:SWA_BLOCK, :]
        if n == 0 and no_prev_bias is not None:
            s_prev = s_prev + no_prev_bias
        s_t = jnp.where(from_prev, s_prev, s_both[SWA_BLOCK:2 * SWA_BLOCK, :])
        sink = jnp.concatenate(
            [jnp.full((1, SWA_BLOCK), sinks[kv * SWA_GROUP + g], F32) for g in range(SWA_GROUP)], axis=1)
        m = jnp.maximum(jnp.max(s_t, axis=0, keepdims=True), sink)
        p_t = jnp.exp2(s_t - m)
        inv = 1.0 / (jnp.sum(p_t, axis=0, keepdims=True) + jnp.exp2(sink - m))
        p_both = jnp.concatenate([jnp.where(from_prev, p_t, 0.0), jnp.where(from_prev, 0.0, p_t)], axis=0)
        return p_both.astype(BF16), inv

    def swa_values(n, kv, p_both, inv):
        hd = slice(kv * SWA_HEAD_DIM, (kv + 1) * SWA_HEAD_DIM)
        vt_both = jnp.concatenate([vt[n][hd, :], vt[n + 1][hd, :]], axis=1)
        o_t = _dot(vt_both, p_both) * inv
        return [o_t[:, g * SWA_BLOCK:(g + 1) * SWA_BLOCK] for g in range(SWA_GROUP)]

    def gla_scores(h):
        att = _dot_nt(q_in_b * head_masks[h], k_in)
        return jnp.where(causal_chunk, att, 0.0).astype(BF16)

    def gla_values(h, att):
        return _dot(att, gvb[:, h * GLA_DV:(h + 1) * GLA_DV])

    steps = [(n, kv) for n in range(n_blocks) for kv in range(SWA_KV_HEADS)]
    assert len(steps) == 4 and GLA_HEADS == 4 and n_chunks == 4

    s0 = swa_scores(*steps[0])
    att0 = gla_scores(0)
    att1 = gla_scores(1)

    state = gla_state[...]
    o_inter = [[] for _ in range(GLA_HEADS)]
    for c in range(n_chunks):
        r0 = c * GLA_CHUNK
        q_c = q_in_b[r0:r0 + GLA_CHUNK, :]
        q_stack = jnp.concatenate([q_c * head_masks[h] for h in range(GLA_HEADS)], axis=0)
        oi = _dot_nt(q_stack, state.astype(BF16))
        for h in range(GLA_HEADS):
            o_inter[h].append(oi[h * GLA_CHUNK:(h + 1) * GLA_CHUNK, :])
        state = decay[r0:r0 + 1, :] * state + upd[c]
    gla_state[...] = state

    s1 = swa_scores(*steps[1])
    o_heads = swa_values(*steps[0], *swa_softmax(*steps[0], s0))
    ov0 = gla_values(0, att0)
    ov1 = gla_values(1, att1)
    att2 = gla_scores(2)
    att3 = gla_scores(3)
    yc_blocks = []
    s2 = swa_scores(*steps[2])
    o_heads += swa_values(*steps[1], *swa_softmax(*steps[1], s1))
    yc_blocks.append(jnp.concatenate(o_heads, axis=0).T)
    gate_slices(1)
    ov2 = gla_values(2, att2)
    ov3 = gla_values(3, att3)
    s3 = swa_scores(*steps[3])
    o_heads = swa_values(*steps[2], *swa_softmax(*steps[2], s2))
    gate_slices(2)
    o_heads += swa_values(*steps[3], *swa_softmax(*steps[3], s3))
    yc_blocks.append(jnp.concatenate(o_heads, axis=0).T)
    yc = jnp.concatenate(yc_blocks, axis=0).astype(BF16)
    k_prev[...] = kr[tok - SWA_BLOCK:tok, :]
    v_prev[...] = vt_f32[-1]

    br_a = _dot(ya, wbr_ref[0])
    gate_slices(3)

    yb_parts = []
    for h, ov in enumerate((ov0, ov1, ov2, ov3)):
        o_h = ov + jnp.concatenate(o_inter[h], axis=0)
        o_h = _rms_scale(o_h) * onw_ref[...]
        r_h = gr[:, h * GLA_DV:(h + 1) * GLA_DV]
        yb_parts.append((o_h * (r_h * jax.nn.sigmoid(r_h))).astype(BF16))
    yb = jnp.concatenate(yb_parts, axis=1)
    br_b = _dot(yb, wbr_ref[1])
    gate_slices(1)

    per_branch = D_MODEL // gate_cols
    assert len(gate_pieces) == N_BRANCHES * per_branch
    acc = x
    mixed_slices = []
    for j in range(per_branch):
        cols = slice(j * gate_cols, (j + 1) * gate_cols)
        br_c_j = _dot(yc, wbr_ref[2, :, cols])
        mixed_j = (gate_pieces[j] * br_a[:, cols] + gate_pieces[per_branch + j] * br_b[:, cols]
                   + gate_pieces[2 * per_branch + j] * br_c_j)
        mixed_slices.append(mixed_j.astype(BF16))
        if j >= 1:
            acc = acc + _dot(mixed_slices[j - 1], wo_ref[(j - 1) * gate_cols:j * gate_cols, :])
    return acc + _dot(mixed_slices[-1], wo_ref[(per_branch - 1) * gate_cols:per_branch * gate_cols, :])


def _cast_mixer_rows(win_t_ref, wbr_f32_ref, wo_f32_ref, wina_ref, winga_ref, winb_ref, wbr_ref, wo_ref):
    wina_ref[...] = win_t_ref[0:IN_A, :].T.astype(BF16)
    lane = lax.broadcasted_iota(jnp.int32, (CAST_ROWS, GA_PAD), 1)
    winga_ref[...] = jnp.where(lane < GLA_RANK, win_t_ref[IN_A:IN_A + GA_PAD, :].T, 0.0).astype(BF16)
    winb_ref[...] = win_t_ref[IN_A + GLA_RANK:N_IN, :].T.astype(BF16)
    wbr_ref[...] = wbr_f32_ref[...].astype(BF16)
    wo_ref[...] = wo_f32_ref[...].astype(BF16)


def _ffn_kernel(x_ref, n2w_ref, wup_ref, wdown_ref, *rest):
    if len(rest) == 1:
        (out_ref,) = rest
    else:
        out_ref = rest[3]
        steps_per_block = pl.num_programs(0) // (D_MODEL // CAST_ROWS)

        @pl.when(pl.program_id(0) % steps_per_block == 0)
        def _():
            _cast_mixer_rows(*rest[:3], *rest[4:])
    x = x_ref[...]
    hb = (_rms_scale(x) * n2w_ref[...]).astype(BF16)
    acc = x
    for c in range(D_FF // FF_CHUNK):
        a = jnp.maximum(_dot(hb, wup_ref[:, c * FF_CHUNK:(c + 1) * FF_CHUNK]), 0.0)
        acc = acc + _dot((a * a).astype(BF16), wdown_ref[c * FF_CHUNK:(c + 1) * FF_CHUNK, :])
    out_ref[...] = acc


def _const_spec(shape):
    zeros = (0,) * len(shape)
    return pl.BlockSpec(shape, lambda *_: zeros, pipeline_mode=pl.Buffered(1))


def _row_block(n_rows, n_steps, dtype):
    rows = n_rows // n_steps
    assert rows * n_steps == n_rows and rows % (2 * SUBLANES if dtype == BF16 else SUBLANES) == 0
    return rows


def _mixer_call(x, cos, sin, p, w_up_f32, w_down_f32, layer, tok):
    bsz, seq, d = x.shape
    n_t = seq // tok
    grid = (bsz, n_t)
    n_steps = bsz * n_t
    tile = lambda b, t: (b, t, 0)
    step_rows = lambda b, t: (b * n_t + t, 0)
    layer_rows = lambda b, t: (layer, b * n_t + t, 0)
    up_rows = _row_block(d, n_steps, BF16)
    down_rows = _row_block(D_FF, n_steps, BF16)
    spec = _const_spec
    in_specs = [
        pl.BlockSpec(memory_space=pltpu.SMEM),
        pl.BlockSpec((1, tok, d), tile),
        pl.BlockSpec((1, tok, LANES), tile),
        pl.BlockSpec((1, tok, LANES), tile),
        spec((1, d)),
        spec((d, IN_A)),
        spec((d, GA_PAD)),
        spec((d, IN_B)),
        spec((CONV_WIDTH, CONV_DIM)),
        spec((1, CONV_DIM)),
        spec((GA_PAD, GLA_HEADS * GLA_DK)),
        spec((1, GLA_HEADS * GLA_DK)),
        spec((1, GLA_DV)),
        spec((1, LANES)),
        spec((1, LANES)),
        spec((N_BRANCHES, BRANCH_WIDTH, d)),
        spec((d, d)),
        pl.BlockSpec((None, up_rows, D_FF), layer_rows),
        pl.BlockSpec((None, down_rows, d), layer_rows),
    ]
    out_specs = [
        pl.BlockSpec((1, tok, d), tile),
        pl.BlockSpec((up_rows, D_FF), step_rows),
        pl.BlockSpec((down_rows, d), step_rows),
    ]
    out_shape = [
        jax.ShapeDtypeStruct(x.shape, F32),
        jax.ShapeDtypeStruct((d, D_FF), BF16),
        jax.ShapeDtypeStruct((D_FF, d), BF16),
    ]
    scratch = [
        pltpu.VMEM((SUB_TILE + SUBLANES, CONV_DIM), F32),
        pltpu.VMEM((GLA_DV, GLA_HEADS * GLA_DK), F32),
        pltpu.VMEM((SWA_BLOCK, SWA_KV_HEADS * SWA_HEAD_DIM), F32),
        pltpu.VMEM((SWA_BLOCK, SWA_KV_HEADS * SWA_HEAD_DIM), F32),
    ]
    out, w_up_b16, w_down_b16 = pl.pallas_call(
        functools.partial(_mixer_kernel, tok=tok),
        grid=grid,
        in_specs=in_specs,
        out_specs=out_specs,
        out_shape=out_shape,
        scratch_shapes=scratch,
        compiler_params=pltpu.CompilerParams(
            dimension_semantics=("arbitrary", "arbitrary"),
            vmem_limit_bytes=V7X_VMEM_BYTES * 3 // 4),
        name="mixer",
    )(p["sinks"], x, cos, sin, p["n1w"], p["w_in_a"], p["w_in_ga"], p["w_in_b"], p["conv_w"], p["conv_b"],
      p["wa2"], p["ba"], p["onw"], p["qnw"], p["knw"], p["w_branch"], p["w_o"], w_up_f32, w_down_f32)
    return out, w_up_b16, w_down_b16


def _mixer_cast_specs(d, layer, block_of):
    n_blocks = d // CAST_ROWS
    br_rows = N_BRANCHES * BRANCH_WIDTH // n_blocks
    assert n_blocks * CAST_ROWS == d and br_rows % (2 * SUBLANES) == 0
    rows_in = lambda *ids: (layer, block_of(*ids), 0)
    cols_in = lambda *ids: (layer, 0, block_of(*ids))
    rows_out = lambda *ids: (block_of(*ids), 0)
    in_specs = [
        pl.BlockSpec((None, N_IN, CAST_ROWS), cols_in),
        pl.BlockSpec((None, br_rows, d), rows_in),
        pl.BlockSpec((None, CAST_ROWS, d), rows_in),
    ]
    out_specs = [
        pl.BlockSpec((CAST_ROWS, IN_A), rows_out),
        pl.BlockSpec((CAST_ROWS, GA_PAD), rows_out),
        pl.BlockSpec((CAST_ROWS, IN_B), rows_out),
        pl.BlockSpec((br_rows, d), rows_out),
        pl.BlockSpec((CAST_ROWS, d), rows_out),
    ]
    out_shape = [
        jax.ShapeDtypeStruct((d, IN_A), BF16),
        jax.ShapeDtypeStruct((d, GA_PAD), BF16),
        jax.ShapeDtypeStruct((d, IN_B), BF16),
        jax.ShapeDtypeStruct((N_BRANCHES * BRANCH_WIDTH, d), BF16),
        jax.ShapeDtypeStruct((d, d), BF16),
    ]
    return in_specs, out_specs, out_shape


def _mixer_cast_operands(w_in, w_branch, w_o):
    d = w_o.shape[-1]
    return [jnp.swapaxes(w_in, 1, 2), w_branch.reshape(-1, N_BRANCHES * BRANCH_WIDTH, d), w_o]


def _mixer_weight_dict(w_in_a, w_in_ga, w_in_b, w_branch, w_o):
    d = w_o.shape[0]
    return {"w_in_a": w_in_a, "w_in_ga": w_in_ga, "w_in_b": w_in_b,
            "w_branch": w_branch.reshape(N_BRANCHES, BRANCH_WIDTH, d), "w_o": w_o}


def _first_mixer_weights(w_in, w_branch, w_o):
    d = w_o.shape[-1]
    in_specs, out_specs, out_shape = _mixer_cast_specs(d, 0, lambda i: i)
    outs = pl.pallas_call(
        _cast_mixer_rows,
        grid=(d // CAST_ROWS,),
        in_specs=in_specs,
        out_specs=out_specs,
        out_shape=out_shape,
        compiler_params=pltpu.CompilerParams(dimension_semantics=("arbitrary",)),
        name="cast_mixer_weights",
    )(*_mixer_cast_operands(w_in, w_branch, w_o))
    return _mixer_weight_dict(*outs)


def _ffn_call(x2, n2w, w_up, w_down, tile, next_mixer_f32=None, next_layer=None):
    n, d = x2.shape
    n_steps = n // tile
    step_rows = lambda i: (i, 0)
    in_specs = [
        pl.BlockSpec((tile, d), step_rows),
        _const_spec((1, d)),
        _const_spec((d, D_FF)),
        _const_spec((D_FF, d)),
    ]
    out_specs = [pl.BlockSpec((tile, d), step_rows)]
    out_shape = [jax.ShapeDtypeStruct(x2.shape, F32)]
    operands = [x2, n2w, w_up, w_down]
    if next_mixer_f32 is not None:
        w_in_f32, w_branch_f32, w_o_f32 = next_mixer_f32
        n_blocks = d // CAST_ROWS
        assert n_steps % n_blocks == 0
        cast_in, cast_out, cast_shape = _mixer_cast_specs(d, next_layer, lambda i: i // (n_steps // n_blocks))
        in_specs += cast_in
        out_specs += cast_out
        out_shape += cast_shape
        operands += _mixer_cast_operands(w_in_f32, w_branch_f32, w_o_f32)
    outs = pl.pallas_call(
        _ffn_kernel,
        grid=(n_steps,),
        in_specs=in_specs,
        out_specs=out_specs,
        out_shape=out_shape,
        compiler_params=pltpu.CompilerParams(
            dimension_semantics=("arbitrary",),
            vmem_limit_bytes=V7X_VMEM_BYTES * 7 // 8),
        name="ffn",
    )(*operands)
    if next_mixer_f32 is None:
        return outs[0], None
    return outs[0], _mixer_weight_dict(*outs[1:])


def _rope_spread_matrix():
    half = ROPE_DIM // 2
    c = np.arange(LANES)[:, None]
    l = np.arange(LANES)[None, :]
    hit = ((c % half) == (l % half)) & ((l % SWA_HEAD_DIM) < ROPE_DIM)
    sign = np.where((l % SWA_HEAD_DIM) < half, -1.0, 1.0)
    m = np.zeros((2 * LANES, 2 * LANES), np.float32)
    m[:LANES, :LANES] = hit
    m[LANES:, LANES:] = hit * sign
    return jnp.asarray(m, BF16)


def _rope_tables(positions):
    bsz, seq = positions.shape
    n = bsz * seq
    tokens = min(ROPE_TILE, n)
    assert n % tokens == 0 and tokens % (ROPE_TOKENS_PER_ROW * SUBLANES) == 0
    half = ROPE_DIM // 2
    inv_freq = ROPE_THETA ** (-jnp.arange(0, ROPE_DIM, 2, dtype=F32) / ROPE_DIM)
    invf_dense = jnp.tile(inv_freq, LANES // half)[None, :]
    pos_dense = jnp.repeat(positions.astype(jnp.int32).reshape(n), half).reshape(n // ROPE_TOKENS_PER_ROW, LANES)
    rows = tokens // ROPE_TOKENS_PER_ROW
    cos, sin = pl.pallas_call(
        functools.partial(_rope_table_kernel, tokens=tokens),
        grid=(n // tokens,),
        in_specs=[
            pl.BlockSpec((rows, LANES), lambda i: (i, 0)),
            _const_spec((1, LANES)),
            _const_spec((2 * LANES, 2 * LANES)),
        ],
        out_specs=[pl.BlockSpec((tokens, LANES), lambda i: (i, 0))] * 2,
        out_shape=[jax.ShapeDtypeStruct((n, LANES), F32)] * 2,
        compiler_params=pltpu.CompilerParams(dimension_semantics=("arbitrary",)),
        name="rope_tables",
    )(pos_dense, invf_dense, _rope_spread_matrix())
    return cos.reshape(bsz, seq, LANES), sin.reshape(bsz, seq, LANES)


def _small_mixer_params(layer, norm1_w, conv_w, conv_b, gla_wa2, gla_ba, gla_onorm_w, q_norm_w, k_norm_w, sinks):
    heads_per_tile = LANES // SWA_HEAD_DIM
    return {
        "sinks": sinks[layer].astype(F32),
        "n1w": norm1_w[layer][None, :],
        "conv_w": conv_w[layer],
        "conv_b": conv_b[layer][None, :],
        "wa2": jnp.pad(gla_wa2[layer], ((0, GA_PAD - GLA_RANK), (0, 0))).astype(BF16),
        "ba": gla_ba[layer][None, :],
        "onw": gla_onorm_w[layer][None, :],
        "qnw": jnp.tile(q_norm_w[layer], heads_per_tile)[None, :],
        "knw": jnp.tile(k_norm_w[layer], heads_per_tile)[None, :],
    }


def kernel(x, positions, norm1_w, w_in, conv_w, conv_b, gla_wa2, gla_ba, gla_onorm_w, q_norm_w, k_norm_w,
           sinks, w_branch, w_o, norm2_w, w_up, w_down):
    bsz, seq, d = x.shape
    depth = w_in.shape[0]
    assert d == D_MODEL and w_in.shape[2] == N_IN
    tok = min(TOK_TILE, seq)
    assert seq % tok == 0 and tok % SUB_TILE == 0 and SUB_TILE % (2 * SWA_BLOCK) == 0
    ffn_tile = min(FFN_TILE, bsz * seq)
    assert (bsz * seq) % ffn_tile == 0
    cos, sin = _rope_tables(positions)
    mixer_w = _first_mixer_weights(w_in, w_branch, w_o)
    for layer in range(depth):
        p = dict(mixer_w, **_small_mixer_params(layer, norm1_w, conv_w, conv_b, gla_wa2, gla_ba, gla_onorm_w,
                                                q_norm_w, k_norm_w, sinks))
        x, w_up_b16, w_down_b16 = _mixer_call(x, cos, sin, p, w_up, w_down, layer, tok)
        nxt = (w_in, w_branch, w_o) if layer + 1 < depth else None
        x2, mixer_w = _ffn_call(x.reshape(bsz * seq, d), norm2_w[layer][None, :], w_up_b16, w_down_b16, ffn_tile,
                                nxt, layer + 1)
        x = x2.reshape(bsz, seq, d)
    return x
```

```python
import functools

import jax
import jax.numpy as jnp
import numpy as np
from jax import lax
from jax.experimental import pallas as pl
from jax.experimental.pallas import tpu as pltpu

F32 = jnp.float32
BF16 = jnp.bfloat16

D_MODEL = 1024
CONV_DIM = 512
CONV_WIDTH = 3
GLA_HEADS = 4
GLA_DK = 64
GLA_DV = 128
GLA_RANK = 16
GLA_TAU = 16.0
GLA_CHUNK = 64
SWA_HEADS = 8
SWA_KV_HEADS = 2
SWA_GROUP = SWA_HEADS // SWA_KV_HEADS
SWA_HEAD_DIM = 64
SWA_BLOCK = 128
ROPE_THETA = 500000.0
ROPE_DIM = SWA_HEAD_DIM // 4
N_BRANCHES = 3
BRANCH_WIDTH = 512
D_FF = 4 * D_MODEL
NORM_EPS = 1e-6
LOG2_E = 1.4426950408889634

LANES = 128
SUBLANES = 8
V7X_VMEM_BYTES = 64 * 1024 * 1024

GA_PAD = LANES
IN_A = 3 * CONV_DIM + 2 * GLA_HEADS * GLA_DK + 2 * GLA_HEADS * GLA_DV
IN_B = SWA_HEADS * SWA_HEAD_DIM + 2 * SWA_KV_HEADS * SWA_HEAD_DIM + N_BRANCHES * D_MODEL
N_IN = IN_A + GLA_RANK + IN_B
A_CONV = 0
A_GQ = A_CONV + 3 * CONV_DIM
A_GK = A_GQ + GLA_HEADS * GLA_DK
A_GV = A_GK + GLA_HEADS * GLA_DK
A_GR = A_GV + GLA_HEADS * GLA_DV
B_SQ = 0
B_SK = B_SQ + SWA_HEADS * SWA_HEAD_DIM
B_SV = B_SK + SWA_KV_HEADS * SWA_HEAD_DIM
B_GATE = B_SV + SWA_KV_HEADS * SWA_HEAD_DIM

ROPE_TOKENS_PER_ROW = LANES // (ROPE_DIM // 2)
ROPE_TILE = 1024
SUB_TILE = 256
TOK_TILE = 512
FFN_TILE = 1024
FF_CHUNK = 1024
CAST_ROWS = 128


def _dot(a, b):
    return jnp.dot(a, b, preferred_element_type=F32)


def _dot_nt(a, b):
    return lax.dot_general(a, b, (((1,), (1,)), ((), ())), preferred_element_type=F32)


def _split_dot(m_bf16, v_f32):
    hi = v_f32.astype(BF16)
    lo = (v_f32 - hi.astype(F32)).astype(BF16)
    return _dot(m_bf16, hi) + _dot(m_bf16, lo)


def _split_dot_rhs(v_f32, m_bf16):
    hi = v_f32.astype(BF16)
    lo = (v_f32 - hi.astype(F32)).astype(BF16)
    return _dot(hi, m_bf16) + _dot(lo, m_bf16)


def _rms_scale(x):
    return x * lax.rsqrt(jnp.mean(x * x, axis=-1, keepdims=True) + NORM_EPS)


def _lane_group_mask(width, group, index):
    lane = lax.broadcasted_iota(jnp.int32, (1, width), 1)
    return jnp.where((lane // group) == index, 1.0, 0.0).astype(F32)


def _split3(v_f32):
    hi = v_f32.astype(BF16)
    r1 = v_f32 - hi.astype(F32)
    mid = r1.astype(BF16)
    lo = (r1 - mid.astype(F32)).astype(BF16)
    return hi, mid, lo


def _cast_mixer_rows(win_t_ref, wbr_f32_ref, wo_f32_ref, wina_ref, winga_ref, winb_ref, wbr_ref, wo_ref):
    wina_ref[...] = win_t_ref[0:IN_A, :].T.astype(BF16)
    lane = lax.broadcasted_iota(jnp.int32, (CAST_ROWS, GA_PAD), 1)
    winga_ref[...] = jnp.where(lane < GLA_RANK, win_t_ref[IN_A:IN_A + GA_PAD, :].T, 0.0).astype(BF16)
    winb_ref[...] = win_t_ref[IN_A + GLA_RANK:N_IN, :].T.astype(BF16)
    wbr_ref[...] = wbr_f32_ref[...].astype(BF16)
    wo_ref[...] = wo_f32_ref[...].astype(BF16)


def _cast_every_few_steps(in_refs, out_refs):
    steps_per_block = pl.num_programs(0) // (D_MODEL // CAST_ROWS)

    @pl.when(pl.program_id(0) % steps_per_block == 0)
    def _():
        _cast_mixer_rows(*in_refs, *out_refs)


def _rope_table_kernel(pos_ref, invf_ref, spread_ref, *rest, tokens):
    cos_ref, sin_ref = rest[3], rest[4]
    _cast_every_few_steps(rest[:3], rest[5:])
    rows = tokens // ROPE_TOKENS_PER_ROW
    ang = pos_ref[...].astype(F32) * invf_ref[...]
    dense = jnp.concatenate([jnp.cos(ang), jnp.sin(ang)], axis=1)
    by_tok = jnp.broadcast_to(dense[:, None, :], (rows, ROPE_TOKENS_PER_ROW, 2 * LANES)).reshape(tokens, 2 * LANES)
    tok_c = lax.broadcasted_iota(jnp.int32, (tokens, 2 * LANES), 0) % ROPE_TOKENS_PER_ROW
    slot_c = (lax.broadcasted_iota(jnp.int32, (tokens, 2 * LANES), 1) % LANES) // (ROPE_DIM // 2)
    own = jnp.where(tok_c == slot_c, by_tok, 0.0)
    out = sum(_dot(piece, spread_ref[...]) for piece in _split3(own))
    lane_in_head = lax.broadcasted_iota(jnp.int32, (1, LANES), 1) % SWA_HEAD_DIM
    cos_ref[...] = out[:, 0:LANES] + jnp.where(lane_in_head >= ROPE_DIM, 1.0, 0.0)
    sin_ref[...] = out[:, LANES:2 * LANES]


def _rope(x, cos, sin_signed, lower_half):
    half = ROPE_DIM // 2
    up = pltpu.roll(x, LANES - half, axis=1)
    down = pltpu.roll(x, half, axis=1)
    return x * cos + jnp.where(lower_half, up, down) * sin_signed


def _mixer_kernel(sinks_ref, x_ref, cos_ref, sin_ref, n1w_ref, wina_ref, winga_ref, winb_ref, convw_ref, convb_ref,
                  wa2_ref, ba_ref, onw_ref, qnw_ref, knw_ref, wbr_ref, wo_ref, wup_f32_ref, wdown_f32_ref,
                  out_ref, wup_ref, wdown_ref, conv_buf, gla_state, k_prev, v_prev, *, tok):
    t = pl.program_id(1)

    @pl.when(t == 0)
    def _():
        conv_buf[0:SUBLANES, :] = jnp.zeros((SUBLANES, CONV_DIM), F32)
        gla_state[...] = jnp.zeros_like(gla_state)
        k_prev[...] = jnp.zeros_like(k_prev)
        v_prev[...] = jnp.zeros_like(v_prev)

    weights = (n1w_ref, wina_ref, winga_ref, winb_ref, convw_ref, convb_ref, wa2_ref, ba_ref, onw_ref, qnw_ref,
               knw_ref, wbr_ref, wo_ref)
    state = (conv_buf, gla_state, k_prev, v_prev)
    sinks = [sinks_ref[h] * LOG2_E for h in range(SWA_HEADS)]
    for s in range(tok // SUB_TILE):
        rows = slice(s * SUB_TILE, (s + 1) * SUB_TILE)
        no_prev_bias = jnp.where(t == 0, -jnp.inf, 0.0) if s == 0 else None
        out_ref[0, rows, :] = _mixer_subtile(x_ref[0, rows, :], cos_ref[0, rows, :], sin_ref[0, rows, :],
                                             no_prev_bias, sinks, weights, state)
    wup_ref[...] = wup_f32_ref[...].astype(BF16)
    wdown_ref[...] = wdown_f32_ref[...].astype(BF16)


def _mixer_subtile(x, cos, sin_signed, no_prev_bias, sinks, weights, state):
    (n1w_ref, wina_ref, winga_ref, winb_ref, convw_ref, convb_ref, wa2_ref, ba_ref, onw_ref, qnw_ref,
     knw_ref, wbr_ref, wo_ref) = weights
    conv_buf, gla_state, k_prev, v_prev = state
    tok = SUB_TILE
    n_chunks = tok // GLA_CHUNK
    n_blocks = tok // SWA_BLOCK

    ub = (_rms_scale(x) * n1w_ref[...]).astype(BF16)

    gate_cols = 2 * LANES
    gate_pieces = []

    def gate_slices(count):
        for _ in range(count):
            c0 = B_GATE + len(gate_pieces) * gate_cols
            gate_pieces.append(jax.nn.sigmoid(_dot(ub, winb_ref[:, c0:c0 + gate_cols])))

    ga = _dot(ub, winga_ref[...])
    pg = _dot(ub, wina_ref[:, A_GQ:IN_A])
    gq = pg[:, 0:A_GK - A_GQ] * (GLA_DK ** -0.5)
    gk = pg[:, A_GK - A_GQ:A_GV - A_GQ]
    gv = pg[:, A_GV - A_GQ:A_GR - A_GQ]
    gr = pg[:, A_GR - A_GQ:IN_A - A_GQ]
    ps = _dot(ub, winb_ref[:, B_SQ:B_GATE])
    sq = ps[:, B_SQ:B_SK]
    sk = ps[:, B_SK:B_SV]
    sv = ps[:, B_SV:B_GATE]

    z = _dot(ga.astype(BF16), wa2_ref[...]) + ba_ref[...]
    glog = (jnp.minimum(z, 0.0) - jnp.log(1.0 + jnp.exp(-jnp.abs(z)))) * (1.0 / GLA_TAU)
    gate_slices(1)

    gi = lax.broadcasted_iota(jnp.int32, (2 * LANES, 2 * LANES), 0) // SWA_HEAD_DIM
    gj = lax.broadcasted_iota(jnp.int32, (2 * LANES, 2 * LANES), 1) // SWA_HEAD_DIM
    head_ones = jnp.where(gi == gj, 1.0, 0.0).astype(BF16)
    n_qcols = SWA_HEADS * SWA_HEAD_DIM // (2 * LANES)
    q_halves = [sq[:, j * 2 * LANES:(j + 1) * 2 * LANES] for j in range(n_qcols)]
    ssq = [_dot((qj * qj).astype(BF16), head_ones) for qj in q_halves]
    ssk = _dot((sk * sk).astype(BF16), head_ones[0:LANES, 0:LANES])

    cx = _dot(ub, wina_ref[:, A_CONV:A_CONV + CONV_DIM])
    cc = _dot(ub, wina_ref[:, A_CONV + 2 * CONV_DIM:A_CONV + 3 * CONV_DIM])
    gate_slices(2)
    cb = _dot(ub, wina_ref[:, A_CONV + CONV_DIM:A_CONV + 2 * CONV_DIM])
    uc = cc * cx
    conv_buf[SUBLANES:SUBLANES + tok, :] = uc
    u1 = conv_buf[SUBLANES - 1:SUBLANES - 1 + tok, :]
    u2 = conv_buf[SUBLANES - 2:SUBLANES - 2 + tok, :]
    yconv = convw_ref[0:1, :] * u2 + convw_ref[1:2, :] * u1 + convw_ref[2:3, :] * uc
    ya = (cb * (yconv + convb_ref[...])).astype(BF16)
    conv_buf[0:SUBLANES, :] = uc[tok - SUBLANES:tok, :]

    ri = lax.broadcasted_iota(jnp.int32, (tok, tok), 0)
    ci = lax.broadcasted_iota(jnp.int32, (tok, tok), 1)
    same_chunk = (ri // GLA_CHUNK) == (ci // GLA_CHUNK)
    causal_chunk = same_chunk & (ci <= ri)
    l_tri = jnp.where(causal_chunk, 1.0, 0.0).astype(BF16)
    b = _split_dot(l_tri, glog)
    b_last = jnp.concatenate(
        [jnp.broadcast_to(b[(c + 1) * GLA_CHUNK - 1:(c + 1) * GLA_CHUNK, :], (GLA_CHUNK, GLA_HEADS * GLA_DK))
         for c in range(n_chunks)], axis=0)
    gate_slices(2)
    q_in_b = (gq * jnp.exp(b)).astype(BF16)
    k_in = (gk * jnp.exp(-b)).astype(BF16)
    k_end_b = (gk * jnp.exp(b_last - b)).astype(BF16)
    decay = jnp.exp(b_last)
    head_masks = [_lane_group_mask(GLA_HEADS * GLA_DK, GLA_DK, h).astype(BF16) for h in range(GLA_HEADS)]
    gvb = gv.astype(BF16)

    lower_half = (lax.broadcasted_iota(jnp.int32, (1, LANES), 1) % SWA_HEAD_DIM) < ROPE_DIM // 2
    q_cols = []
    for j in range(n_qcols):
        qn = q_halves[j] * lax.rsqrt(ssq[j] * (1.0 / SWA_HEAD_DIM) + NORM_EPS)
        for i in range(2):
            qn_i = qn[:, i * LANES:(i + 1) * LANES] * qnw_ref[...]
            q_cols.append((_rope(qn_i, cos, sin_signed, lower_half) * (SWA_HEAD_DIM ** -0.5 * LOG2_E)).astype(BF16))
    kn = sk * lax.rsqrt(ssk * (1.0 / SWA_HEAD_DIM) + NORM_EPS) * knw_ref[...]
    kr = _rope(kn, cos, sin_signed, lower_half)

    upd = []
    for c in range(n_chunks):
        r0 = c * GLA_CHUNK
        ke_c = k_end_b[r0:r0 + GLA_CHUNK, :]
        k_stack = jnp.concatenate([ke_c * head_masks[h] for h in range(GLA_HEADS)], axis=0)
        v_stack = jnp.concatenate(
            [gv[r0:r0 + GLA_CHUNK, h * GLA_DV:(h + 1) * GLA_DV] for h in range(GLA_HEADS)], axis=0)
        upd.append(_dot(v_stack.T.astype(BF16), k_stack))

    group_masks = [_lane_group_mask(2 * LANES, SWA_HEAD_DIM, g).astype(BF16) for g in range(SWA_GROUP)]
    lane128 = lax.broadcasted_iota(jnp.int32, (1, LANES), 1)
    slot_j = lax.broadcasted_iota(jnp.int32, (SWA_BLOCK, SWA_GROUP * SWA_BLOCK), 0)
    qry_i = lax.broadcasted_iota(jnp.int32, (SWA_BLOCK, SWA_GROUP * SWA_BLOCK), 1) % SWA_BLOCK
    from_prev = slot_j > qry_i

    def kv_keys(k_blk):
        k_sw = pltpu.roll(k_blk, SWA_HEAD_DIM, axis=1)
        reps = []
        for kv in range(SWA_KV_HEADS):
            k_one = jnp.where((lane128 // SWA_HEAD_DIM) == kv, k_blk, k_sw).astype(BF16)
            reps.append(jnp.concatenate([k_one, k_one], axis=1))
        return reps

    keys = [kv_keys(k_prev[...])] + [kv_keys(kr[n * SWA_BLOCK:(n + 1) * SWA_BLOCK, :]) for n in range(n_blocks)]
    vt_f32 = [sv[n * SWA_BLOCK:(n + 1) * SWA_BLOCK, :].T for n in range(n_blocks)]
    vt = [v_prev[...].astype(BF16)] + [v.astype(BF16) for v in vt_f32]

    def swa_scores(n, kv):
        r0 = n * SWA_BLOCK
        q_kv = jnp.concatenate(q_cols[2 * kv:2 * kv + 2], axis=1)[r0:r0 + SWA_BLOCK, :]
        q_stack = jnp.concatenate([q_kv * group_masks[g] for g in range(SWA_GROUP)], axis=0)
        k_both = jnp.concatenate([keys[n][kv], keys[n + 1][kv]], axis=0)
        return _dot_nt(k_both, q_stack)

    def swa_softmax(n, kv, s_both):
        s_prev = s_both[0:SWA_BLOCK, :]
        if n == 0 and no_prev_bias is not None:
            s_prev = s_prev + no_prev_bias
        s_t = jnp.where(from_prev, s_prev, s_both[SWA_BLOCK:2 * SWA_BLOCK, :])
        sink = jnp.concatenate(
            [jnp.full((1, SWA_BLOCK), sinks[kv * SWA_GROUP + g], F32) for g in range(SWA_GROUP)], axis=1)
        m = jnp.maximum(jnp.max(s_t, axis=0, keepdims=True), sink)
        p_t = jnp.exp2(s_t - m)
        inv = 1.0 / (jnp.sum(p_t, axis=0, keepdims=True) + jnp.exp2(sink - m))
        p_both = jnp.concatenate([jnp.where(from_prev, p_t, 0.0), jnp.where(from_prev, 0.0, p_t)], axis=0)
        return p_both.astype(BF16), inv

    def swa_values(n, kv, p_both, inv):
        hd = slice(kv * SWA_HEAD_DIM, (kv + 1) * SWA_HEAD_DIM)
        vt_both = jnp.concatenate([vt[n][hd, :], vt[n + 1][hd, :]], axis=1)
        o_t = _dot(vt_both, p_both) * inv
        return [o_t[:, g * SWA_BLOCK:(g + 1) * SWA_BLOCK] for g in range(SWA_GROUP)]

    def gla_scores(h):
        att = _dot_nt(q_in_b * head_masks[h], k_in)
        return jnp.where(causal_chunk, att, 0.0).astype(BF16)

    def gla_values(h, att):
        return _dot(att, gvb[:, h * GLA_DV:(h + 1) * GLA_DV])

    steps = [(n, kv) for n in range(n_blocks) for kv in range(SWA_KV_HEADS)]
    assert len(steps) == 4 and GLA_HEADS == 4 and n_chunks == 4

    s0 = swa_scores(*steps[0])
    att0 = gla_scores(0)
    att1 = gla_scores(1)

    state = gla_state[...]
    o_inter = [[] for _ in range(GLA_HEADS)]
    for c in range(n_chunks):
        r0 = c * GLA_CHUNK
        q_c = q_in_b[r0:r0 + GLA_CHUNK, :]
        q_stack = jnp.concatenate([q_c * head_masks[h] for h in range(GLA_HEADS)], axis=0)
        oi = _dot_nt(q_stack, state.astype(BF16))
        for h in range(GLA_HEADS):
            o_inter[h].append(oi[h * GLA_CHUNK:(h + 1) * GLA_CHUNK, :])
        state = decay[r0:r0 + 1, :] * state + upd[c]
    gla_state[...] = state

    s1 = swa_scores(*steps[1])
    o_heads = swa_values(*steps[0], *swa_softmax(*steps[0], s0))
    ov0 = gla_values(0, att0)
    ov1 = gla_values(1, att1)
    att2 = gla_scores(2)
    att3 = gla_scores(3)
    yc_blocks = []
    s2 = swa_scores(*steps[2])
    o_heads += swa_values(*steps[1], *swa_softmax(*steps[1], s1))
    yc_blocks.append(jnp.concatenate(o_heads, axis=0).T)
    gate_slices(1)
    ov2 = gla_values(2, att2)
    ov3 = gla_values(3, att3)
    s3 = swa_scores(*steps[3])
    o_heads = swa_values(*steps[2], *swa_softmax(*steps[2], s2))
    gate_slices(2)
    o_heads += swa_values(*steps[3], *swa_softmax(*steps[3], s3))
    yc_blocks.append(jnp.concatenate(o_heads, axis=0).T)
    yc = jnp.concatenate(yc_blocks, axis=0).astype(BF16)
    k_prev[...] = kr[tok - SWA_BLOCK:tok, :]
    v_prev[...] = vt_f32[-1]

    br_a = _dot(ya, wbr_ref[0])
    gate_slices(3)

    yb_parts = []
    for h, ov in enumerate((ov0, ov1, ov2, ov3)):
        o_h = ov + jnp.concatenate(o_inter[h], axis=0)
        o_h = _rms_scale(o_h) * onw_ref[...]
        r_h = gr[:, h * GLA_DV:(h + 1) * GLA_DV]
        yb_parts.append((o_h * (r_h * jax.nn.sigmoid(r_h))).astype(BF16))
    yb = jnp.concatenate(yb_parts, axis=1)
    br_b = _dot(yb, wbr_ref[1])
    gate_slices(1)

    per_branch = D_MODEL // gate_cols
    assert len(gate_pieces) == N_BRANCHES * per_branch
    acc = x
    mixed_slices = []
    for j in range(per_branch):
        cols = slice(j * gate_cols, (j + 1) * gate_cols)
        br_c_j = _dot(yc, wbr_ref[2, :, cols])
        mixed_j = (gate_pieces[j] * br_a[:, cols] + gate_pieces[per_branch + j] * br_b[:, cols]
                   + gate_pieces[2 * per_branch + j] * br_c_j)
        mixed_slices.append(mixed_j.astype(BF16))
        if j >= 1:
            acc = acc + _dot(mixed_slices[j - 1], wo_ref[(j - 1) * gate_cols:j * gate_cols, :])
    return acc + _dot(mixed_slices[-1], wo_ref[(per_branch - 1) * gate_cols:per_branch * gate_cols, :])


def _ffn_kernel(x_ref, n2w_ref, wup_ref, wdown_ref, *rest):
    if len(rest) == 1:
        (out_ref,) = rest
    else:
        out_ref = rest[3]
        _cast_mixer_rows(*rest[:3], *rest[4:])
    x = x_ref[...]
    hb = (_rms_scale(x) * n2w_ref[...]).astype(BF16)
    acc = x
    for c in range(D_FF // FF_CHUNK):
        a = jnp.maximum(_dot(hb, wup_ref[:, c * FF_CHUNK:(c + 1) * FF_CHUNK]), 0.0)
        acc = acc + _dot((a * a).astype(BF16), wdown_ref[c * FF_CHUNK:(c + 1) * FF_CHUNK, :])
    out_ref[...] = acc


def _const_spec(shape):
    zeros = (0,) * len(shape)
    return pl.BlockSpec(shape, lambda *_: zeros, pipeline_mode=pl.Buffered(1))


def _row_block(n_rows, n_steps, dtype):
    rows = n_rows // n_steps
    assert rows * n_steps == n_rows and rows % (2 * SUBLANES if dtype == BF16 else SUBLANES) == 0
    return rows


def _mixer_call(x, cos, sin, p, w_up_f32, w_down_f32, layer, tok):
    bsz, seq, d = x.shape
    n_t = seq // tok
    grid = (bsz, n_t)
    n_steps = bsz * n_t
    tile = lambda b, t: (b, t, 0)
    step_rows = lambda b, t: (b * n_t + t, 0)
    layer_rows = lambda b, t: (layer, b * n_t + t, 0)
    up_rows = _row_block(d, n_steps, BF16)
    down_rows = _row_block(D_FF, n_steps, BF16)
    spec = _const_spec
    in_specs = [
        pl.BlockSpec(memory_space=pltpu.SMEM),
        pl.BlockSpec((1, tok, d), tile),
        pl.BlockSpec((1, tok, LANES), tile),
        pl.BlockSpec((1, tok, LANES), tile),
        spec((1, d)),
        spec((d, IN_A)),
        spec((d, GA_PAD)),
        spec((d, IN_B)),
        spec((CONV_WIDTH, CONV_DIM)),
        spec((1, CONV_DIM)),
        spec((GA_PAD, GLA_HEADS * GLA_DK)),
        spec((1, GLA_HEADS * GLA_DK)),
        spec((1, GLA_DV)),
        spec((1, LANES)),
        spec((1, LANES)),
        spec((N_BRANCHES, BRANCH_WIDTH, d)),
        spec((d, d)),
        pl.BlockSpec((None, up_rows, D_FF), layer_rows),
        pl.BlockSpec((None, down_rows, d), layer_rows),
    ]
    out_specs = [
        pl.BlockSpec((1, tok, d), tile),
        pl.BlockSpec((up_rows, D_FF), step_rows),
        pl.BlockSpec((down_rows, d), step_rows),
    ]
    out_shape = [
        jax.ShapeDtypeStruct(x.shape, F32),
        jax.ShapeDtypeStruct((d, D_FF), BF16),
        jax.ShapeDtypeStruct((D_FF, d), BF16),
    ]
    scratch = [
        pltpu.VMEM((SUB_TILE + SUBLANES, CONV_DIM), F32),
        pltpu.VMEM((GLA_DV, GLA_HEADS * GLA_DK), F32),
        pltpu.VMEM((SWA_BLOCK, SWA_KV_HEADS * SWA_HEAD_DIM), F32),
        pltpu.VMEM((SWA_BLOCK, SWA_KV_HEADS * SWA_HEAD_DIM), F32),
    ]
    out, w_up_b16, w_down_b16 = pl.pallas_call(
        functools.partial(_mixer_kernel, tok=tok),
        grid=grid,
        in_specs=in_specs,
        out_specs=out_specs,
        out_shape=out_shape,
        scratch_shapes=scratch,
        compiler_params=pltpu.CompilerParams(
            dimension_semantics=("arbitrary", "arbitrary"),
            vmem_limit_bytes=V7X_VMEM_BYTES * 3 // 4),
        name="mixer",
    )(p["sinks"], x, cos, sin, p["n1w"], p["w_in_a"], p["w_in_ga"], p["w_in_b"], p["conv_w"], p["conv_b"],
      p["wa2"], p["ba"], p["onw"], p["qnw"], p["knw"], p["w_branch"], p["w_o"], w_up_f32, w_down_f32)
    return out, w_up_b16, w_down_b16


def _mixer_cast_specs(d, layer, block_of):
    n_blocks = d // CAST_ROWS
    br_rows = N_BRANCHES * BRANCH_WIDTH // n_blocks
    assert n_blocks * CAST_ROWS == d and br_rows % (2 * SUBLANES) == 0
    rows_in = lambda *ids: (layer, block_of(*ids), 0)
    cols_in = lambda *ids: (layer, 0, block_of(*ids))
    rows_out = lambda *ids: (block_of(*ids), 0)
    in_specs = [
        pl.BlockSpec((None, N_IN, CAST_ROWS), cols_in),
        pl.BlockSpec((None, br_rows, d), rows_in),
        pl.BlockSpec((None, CAST_ROWS, d), rows_in),
    ]
    out_specs = [
        pl.BlockSpec((CAST_ROWS, IN_A), rows_out),
        pl.BlockSpec((CAST_ROWS, GA_PAD), rows_out),
        pl.BlockSpec((CAST_ROWS, IN_B), rows_out),
        pl.BlockSpec((br_rows, d), rows_out),
        pl.BlockSpec((CAST_ROWS, d), rows_out),
    ]
    out_shape = [
        jax.ShapeDtypeStruct((d, IN_A), BF16),
        jax.ShapeDtypeStruct((d, GA_PAD), BF16),
        jax.ShapeDtypeStruct((d, IN_B), BF16),
        jax.ShapeDtypeStruct((N_BRANCHES * BRANCH_WIDTH, d), BF16),
        jax.ShapeDtypeStruct((d, d), BF16),
    ]
    return in_specs, out_specs, out_shape


def _mixer_cast_operands(w_in, w_branch, w_o):
    d = w_o.shape[-1]
    return [jnp.swapaxes(w_in, 1, 2), w_branch.reshape(-1, N_BRANCHES * BRANCH_WIDTH, d), w_o]


def _mixer_weight_dict(w_in_a, w_in_ga, w_in_b, w_branch, w_o):
    d = w_o.shape[0]
    return {"w_in_a": w_in_a, "w_in_ga": w_in_ga, "w_in_b": w_in_b,
            "w_branch": w_branch.reshape(N_BRANCHES, BRANCH_WIDTH, d), "w_o": w_o}


def _ffn_call(x2, n2w, w_up, w_down, tile, next_mixer_f32=None, next_layer=None):
    n, d = x2.shape
    n_steps = n // tile
    step_rows = lambda i: (i, 0)
    in_specs = [
        pl.BlockSpec((tile, d), step_rows),
        _const_spec((1, d)),
        _const_spec((d, D_FF)),
        _const_spec((D_FF, d)),
    ]
    out_specs = [pl.BlockSpec((tile, d), step_rows)]
    out_shape = [jax.ShapeDtypeStruct(x2.shape, F32)]
    operands = [x2, n2w, w_up, w_down]
    if next_mixer_f32 is not None:
        w_in_f32, w_branch_f32, w_o_f32 = next_mixer_f32
        n_blocks = d // CAST_ROWS
        assert n_steps % n_blocks == 0
        cast_in, cast_out, cast_shape = _mixer_cast_specs(d, next_layer, lambda i: i // (n_steps // n_blocks))
        in_specs += cast_in
        out_specs += cast_out
        out_shape += cast_shape
        operands += _mixer_cast_operands(w_in_f32, w_branch_f32, w_o_f32)
    outs = pl.pallas_call(
        _ffn_kernel,
        grid=(n_steps,),
        in_specs=in_specs,
        out_specs=out_specs,
        out_shape=out_shape,
        compiler_params=pltpu.CompilerParams(
            dimension_semantics=("arbitrary",),
            vmem_limit_bytes=V7X_VMEM_BYTES * 7 // 8),
        name="ffn",
    )(*operands)
    if next_mixer_f32 is None:
        return outs[0], None
    return outs[0], _mixer_weight_dict(*outs[1:])


def _rope_spread_matrix():
    half = ROPE_DIM // 2
    c = np.arange(LANES)[:, None]
    l = np.arange(LANES)[None, :]
    hit = ((c % half) == (l % half)) & ((l % SWA_HEAD_DIM) < ROPE_DIM)
    sign = np.where((l % SWA_HEAD_DIM) < half, -1.0, 1.0)
    m = np.zeros((2 * LANES, 2 * LANES), np.float32)
    m[:LANES, :LANES] = hit
    m[LANES:, LANES:] = hit * sign
    return jnp.asarray(m, BF16)


def _rope_tables_and_first_weights(positions, w_in, w_branch, w_o):
    bsz, seq = positions.shape
    d = w_o.shape[-1]
    n = bsz * seq
    tokens = min(ROPE_TILE, n)
    assert n % tokens == 0 and tokens % (ROPE_TOKENS_PER_ROW * SUBLANES) == 0
    half = ROPE_DIM // 2
    inv_freq = ROPE_THETA ** (-jnp.arange(0, ROPE_DIM, 2, dtype=F32) / ROPE_DIM)
    invf_dense = jnp.tile(inv_freq, LANES // half)[None, :]
    pos_dense = jnp.repeat(positions.astype(jnp.int32).reshape(n), half).reshape(n // ROPE_TOKENS_PER_ROW, LANES)
    rows = tokens // ROPE_TOKENS_PER_ROW
    n_steps = n // tokens
    n_blocks = d // CAST_ROWS
    assert n_steps % n_blocks == 0
    cast_in, cast_out, cast_shape = _mixer_cast_specs(d, 0, lambda i: i // (n_steps // n_blocks))
    outs = pl.pallas_call(
        functools.partial(_rope_table_kernel, tokens=tokens),
        grid=(n_steps,),
        in_specs=[
            pl.BlockSpec((rows, LANES), lambda i: (i, 0)),
            _const_spec((1, LANES)),
            _const_spec((2 * LANES, 2 * LANES)),
        ] + cast_in,
        out_specs=[pl.BlockSpec((tokens, LANES), lambda i: (i, 0))] * 2 + cast_out,
        out_shape=[jax.ShapeDtypeStruct((n, LANES), F32)] * 2 + cast_shape,
        compiler_params=pltpu.CompilerParams(dimension_semantics=("arbitrary",)),
        name="rope_tables",
    )(pos_dense, invf_dense, _rope_spread_matrix(), *_mixer_cast_operands(w_in, w_branch, w_o))
    cos, sin = outs[0], outs[1]
    return cos.reshape(bsz, seq, LANES), sin.reshape(bsz, seq, LANES), _mixer_weight_dict(*outs[2:])


def _small_mixer_params(layer, norm1_w, conv_w, conv_b, gla_wa2, gla_ba, gla_onorm_w, q_norm_w, k_norm_w, sinks):
    heads_per_tile = LANES // SWA_HEAD_DIM
    return {
        "sinks": sinks[layer].astype(F32),
        "n1w": norm1_w[layer][None, :],
        "conv_w": conv_w[layer],
        "conv_b": conv_b[layer][None, :],
        "wa2": jnp.pad(gla_wa2[layer], ((0, GA_PAD - GLA_RANK), (0, 0))).astype(BF16),
        "ba": gla_ba[layer][None, :],
        "onw": gla_onorm_w[layer][None, :],
        "qnw": jnp.tile(q_norm_w[layer], heads_per_tile)[None, :],
        "knw": jnp.tile(k_norm_w[layer], heads_per_tile)[None, :],
    }


def kernel(x, positions, norm1_w, w_in, conv_w, conv_b, gla_wa2, gla_ba, gla_onorm_w, q_norm_w, k_norm_w,
           sinks, w_branch, w_o, norm2_w, w_up, w_down):
    bsz, seq, d = x.shape
    depth = w_in.shape[0]
    assert d == D_MODEL and w_in.shape[2] == N_IN
    tok = min(TOK_TILE, seq)
    assert seq % tok == 0 and tok % SUB_TILE == 0 and SUB_TILE % (2 * SWA_BLOCK) == 0
    ffn_tile = min(FFN_TILE, bsz * seq)
    assert (bsz * seq) % ffn_tile == 0
    cos, sin, mixer_w = _rope_tables_and_first_weights(positions, w_in, w_branch, w_o)
    for layer in range(depth):
        p = dict(mixer_w, **_small_mixer_params(layer, norm1_w, conv_w, conv_b, gla_wa2, gla_ba, gla_onorm_w,
                                                q_norm_w, k_norm_w, sinks))
        x, w_up_b16, w_down_b16 = _mixer_call(x, cos, sin, p, w_up, w_down, layer, tok)
        nxt = (w_in, w_branch, w_o) if layer + 1 < depth else None
        x2, mixer_w = _ffn_call(x.reshape(bsz * seq, d), norm2_w[layer][None, :], w_up_b16, w_down_b16, ffn_tile,
                                nxt, layer + 1)
        x = x2.reshape(bsz, seq, d)
    return x
```

```python
import functools

import jax
import jax.numpy as jnp
import numpy as np
from jax import lax
from jax.experimental import pallas as pl
from jax.experimental.pallas import tpu as pltpu

F32 = jnp.float32
BF16 = jnp.bfloat16

D_MODEL = 1024
CONV_DIM = 512
CONV_WIDTH = 3
GLA_HEADS = 4
GLA_DK = 64
GLA_DV = 128
GLA_RANK = 16
GLA_TAU = 16.0
GLA_CHUNK = 64
SWA_HEADS = 8
SWA_KV_HEADS = 2
SWA_GROUP = SWA_HEADS // SWA_KV_HEADS
SWA_HEAD_DIM = 64
SWA_BLOCK = 128
ROPE_THETA = 500000.0
ROPE_DIM = SWA_HEAD_DIM // 4
N_BRANCHES = 3
BRANCH_WIDTH = 512
D_FF = 4 * D_MODEL
NORM_EPS = 1e-6
LOG2_E = 1.4426950408889634

LANES = 128
SUBLANES = 8
V7X_VMEM_BYTES = 64 * 1024 * 1024

GA_PAD = LANES
IN_A = 3 * CONV_DIM + 2 * GLA_HEADS * GLA_DK + 2 * GLA_HEADS * GLA_DV
IN_B = SWA_HEADS * SWA_HEAD_DIM + 2 * SWA_KV_HEADS * SWA_HEAD_DIM + N_BRANCHES * D_MODEL
N_IN = IN_A + GLA_RANK + IN_B
A_CONV = 0
A_GQ = A_CONV + 3 * CONV_DIM
A_GK = A_GQ + GLA_HEADS * GLA_DK
A_GV = A_GK + GLA_HEADS * GLA_DK
A_GR = A_GV + GLA_HEADS * GLA_DV
B_SQ = 0
B_SK = B_SQ + SWA_HEADS * SWA_HEAD_DIM
B_SV = B_SK + SWA_KV_HEADS * SWA_HEAD_DIM
B_GATE = B_SV + SWA_KV_HEADS * SWA_HEAD_DIM

ROPE_TOKENS_PER_ROW = LANES // (ROPE_DIM // 2)
SUB_TILE = 256
TOK_TILE = 512
FFN_TILE = 1024
FF_CHUNK = 1024
CAST_ROWS = 128


def _dot(a, b):
    return jnp.dot(a, b, preferred_element_type=F32)


def _dot_nt(a, b):
    return lax.dot_general(a, b, (((1,), (1,)), ((), ())), preferred_element_type=F32)


def _split_dot(m_bf16, v_f32):
    hi = v_f32.astype(BF16)
    lo = (v_f32 - hi.astype(F32)).astype(BF16)
    return _dot(m_bf16, hi) + _dot(m_bf16, lo)


def _split_dot_rhs(v_f32, m_bf16):
    hi = v_f32.astype(BF16)
    lo = (v_f32 - hi.astype(F32)).astype(BF16)
    return _dot(hi, m_bf16) + _dot(lo, m_bf16)


def _rms_scale(x):
    return x * lax.rsqrt(jnp.mean(x * x, axis=-1, keepdims=True) + NORM_EPS)


def _lane_group_mask(width, group, index):
    lane = lax.broadcasted_iota(jnp.int32, (1, width), 1)
    return jnp.where((lane // group) == index, 1.0, 0.0).astype(F32)


def _split3(v_f32):
    hi = v_f32.astype(BF16)
    r1 = v_f32 - hi.astype(F32)
    mid = r1.astype(BF16)
    lo = (r1 - mid.astype(F32)).astype(BF16)
    return hi, mid, lo


def _cast_mixer_rows(win_t_ref, wbr_f32_ref, wo_f32_ref, wina_ref, winga_ref, winb_ref, wbr_ref, wo_ref):
    wina_ref[...] = win_t_ref[0:IN_A, :].T.astype(BF16)
    lane = lax.broadcasted_iota(jnp.int32, (CAST_ROWS, GA_PAD), 1)
    winga_ref[...] = jnp.where(lane < GLA_RANK, win_t_ref[IN_A:IN_A + GA_PAD, :].T, 0.0).astype(BF16)
    winb_ref[...] = win_t_ref[IN_A + GLA_RANK:N_IN, :].T.astype(BF16)
    wbr_ref[...] = wbr_f32_ref[...].astype(BF16)
    wo_ref[...] = wo_f32_ref[...].astype(BF16)


def _rope_table_kernel(pos_ref, invf_ref, spread_ref, *rest, tokens):
    cos_ref, sin_ref = rest[3], rest[4]
    _cast_mixer_rows(*rest[:3], *rest[5:])
    rows = tokens // ROPE_TOKENS_PER_ROW
    ang = pos_ref[...].astype(F32) * invf_ref[...]
    dense = jnp.concatenate([jnp.cos(ang), jnp.sin(ang)], axis=1)
    by_tok = jnp.broadcast_to(dense[:, None, :], (rows, ROPE_TOKENS_PER_ROW, 2 * LANES)).reshape(tokens, 2 * LANES)
    tok_c = lax.broadcasted_iota(jnp.int32, (tokens, 2 * LANES), 0) % ROPE_TOKENS_PER_ROW
    slot_c = (lax.broadcasted_iota(jnp.int32, (tokens, 2 * LANES), 1) % LANES) // (ROPE_DIM // 2)
    own = jnp.where(tok_c == slot_c, by_tok, 0.0)
    out = sum(_dot(piece, spread_ref[...]) for piece in _split3(own))
    lane_in_head = lax.broadcasted_iota(jnp.int32, (1, LANES), 1) % SWA_HEAD_DIM
    cos_ref[...] = out[:, 0:LANES] + jnp.where(lane_in_head >= ROPE_DIM, 1.0, 0.0)
    sin_ref[...] = out[:, LANES:2 * LANES]


def _rope(x, cos, sin_signed, lower_half):
    half = ROPE_DIM // 2
    up = pltpu.roll(x, LANES - half, axis=1)
    down = pltpu.roll(x, half, axis=1)
    return x * cos + jnp.where(lower_half, up, down) * sin_signed


def _mixer_kernel(sinks_ref, x_ref, cos_ref, sin_ref, n1w_ref, wina_ref, winga_ref, winb_ref, convw_ref, convb_ref,
                  wa2_ref, ba_ref, onw_ref, qnw_ref, knw_ref, wbr_ref, wo_ref, wup_f32_ref, wdown_f32_ref,
                  out_ref, wup_ref, wdown_ref, conv_buf, gla_state, k_prev, v_prev, *, tok):
    t = pl.program_id(1)

    @pl.when(t == 0)
    def _():
        conv_buf[0:SUBLANES, :] = jnp.zeros((SUBLANES, CONV_DIM), F32)
        gla_state[...] = jnp.zeros_like(gla_state)
        k_prev[...] = jnp.zeros_like(k_prev)
        v_prev[...] = jnp.zeros_like(v_prev)

    weights = (n1w_ref, wina_ref, winga_ref, winb_ref, convw_ref, convb_ref, wa2_ref, ba_ref, onw_ref, qnw_ref,
               knw_ref, wbr_ref, wo_ref)
    state = (conv_buf, gla_state, k_prev, v_prev)
    sinks = [sinks_ref[h] * LOG2_E for h in range(SWA_HEADS)]
    for s in range(tok // SUB_TILE):
        rows = slice(s * SUB_TILE, (s + 1) * SUB_TILE)
        no_prev_bias = jnp.where(t == 0, -jnp.inf, 0.0) if s == 0 else None
        out_ref[0, rows, :] = _mixer_subtile(x_ref[0, rows, :], cos_ref[0, rows, :], sin_ref[0, rows, :],
                                             no_prev_bias, sinks, weights, state)
    wup_ref[...] = wup_f32_ref[...].astype(BF16)
    wdown_ref[...] = wdown_f32_ref[...].astype(BF16)


def _mixer_subtile(x, cos, sin_signed, no_prev_bias, sinks, weights, state):
    (n1w_ref, wina_ref, winga_ref, winb_ref, convw_ref, convb_ref, wa2_ref, ba_ref, onw_ref, qnw_ref,
     knw_ref, wbr_ref, wo_ref) = weights
    conv_buf, gla_state, k_prev, v_prev = state
    tok = SUB_TILE
    n_chunks = tok // GLA_CHUNK
    n_blocks = tok // SWA_BLOCK

    ub = (_rms_scale(x) * n1w_ref[...]).astype(BF16)

    gate_cols = 2 * LANES
    gate_pieces = []

    def gate_slices(count):
        for _ in range(count):
            c0 = B_GATE + len(gate_pieces) * gate_cols
            gate_pieces.append(jax.nn.sigmoid(_dot(ub, winb_ref[:, c0:c0 + gate_cols])))

    ga = _dot(ub, winga_ref[...])
    pg = _dot(ub, wina_ref[:, A_GQ:IN_A])
    gq = pg[:, 0:A_GK - A_GQ] * (GLA_DK ** -0.5)
    gk = pg[:, A_GK - A_GQ:A_GV - A_GQ]
    gv = pg[:, A_GV - A_GQ:A_GR - A_GQ]
    gr = pg[:, A_GR - A_GQ:IN_A - A_GQ]
    ps = _dot(ub, winb_ref[:, B_SQ:B_GATE])
    sq = ps[:, B_SQ:B_SK]
    sk = ps[:, B_SK:B_SV]
    sv = ps[:, B_SV:B_GATE]

    z = _dot(ga.astype(BF16), wa2_ref[...]) + ba_ref[...]
    glog = (jnp.minimum(z, 0.0) - jnp.log(1.0 + jnp.exp(-jnp.abs(z)))) * (1.0 / GLA_TAU)
    gate_slices(1)

    gi = lax.broadcasted_iota(jnp.int32, (2 * LANES, 2 * LANES), 0) // SWA_HEAD_DIM
    gj = lax.broadcasted_iota(jnp.int32, (2 * LANES, 2 * LANES), 1) // SWA_HEAD_DIM
    head_ones = jnp.where(gi == gj, 1.0, 0.0).astype(BF16)
    n_qcols = SWA_HEADS * SWA_HEAD_DIM // (2 * LANES)
    q_halves = [sq[:, j * 2 * LANES:(j + 1) * 2 * LANES] for j in range(n_qcols)]
    ssq = [_dot((qj * qj).astype(BF16), head_ones) for qj in q_halves]
    ssk = _dot((sk * sk).astype(BF16), head_ones[0:LANES, 0:LANES])

    cx = _dot(ub, wina_ref[:, A_CONV:A_CONV + CONV_DIM])
    cc = _dot(ub, wina_ref[:, A_CONV + 2 * CONV_DIM:A_CONV + 3 * CONV_DIM])
    gate_slices(2)
    cb = _dot(ub, wina_ref[:, A_CONV + CONV_DIM:A_CONV + 2 * CONV_DIM])
    uc = cc * cx
    conv_buf[SUBLANES:SUBLANES + tok, :] = uc
    u1 = conv_buf[SUBLANES - 1:SUBLANES - 1 + tok, :]
    u2 = conv_buf[SUBLANES - 2:SUBLANES - 2 + tok, :]
    yconv = convw_ref[0:1, :] * u2 + convw_ref[1:2, :] * u1 + convw_ref[2:3, :] * uc
    ya = (cb * (yconv + convb_ref[...])).astype(BF16)
    conv_buf[0:SUBLANES, :] = uc[tok - SUBLANES:tok, :]

    ri = lax.broadcasted_iota(jnp.int32, (tok, tok), 0)
    ci = lax.broadcasted_iota(jnp.int32, (tok, tok), 1)
    same_chunk = (ri // GLA_CHUNK) == (ci // GLA_CHUNK)
    causal_chunk = same_chunk & (ci <= ri)
    l_tri = jnp.where(causal_chunk, 1.0, 0.0).astype(BF16)
    b = _split_dot(l_tri, glog)
    b_last = jnp.concatenate(
        [jnp.broadcast_to(b[(c + 1) * GLA_CHUNK - 1:(c + 1) * GLA_CHUNK, :], (GLA_CHUNK, GLA_HEADS * GLA_DK))
         for c in range(n_chunks)], axis=0)
    gate_slices(2)
    q_in_b = (gq * jnp.exp(b)).astype(BF16)
    k_in = (gk * jnp.exp(-b)).astype(BF16)
    k_end_b = (gk * jnp.exp(b_last - b)).astype(BF16)
    decay = jnp.exp(b_last)
    head_masks = [_lane_group_mask(GLA_HEADS * GLA_DK, GLA_DK, h).astype(BF16) for h in range(GLA_HEADS)]
    gvb = gv.astype(BF16)

    lower_half = (lax.broadcasted_iota(jnp.int32, (1, LANES), 1) % SWA_HEAD_DIM) < ROPE_DIM // 2
    q_cols = []
    for j in range(n_qcols):
        qn = q_halves[j] * lax.rsqrt(ssq[j] * (1.0 / SWA_HEAD_DIM) + NORM_EPS)
        for i in range(2):
            qn_i = qn[:, i * LANES:(i + 1) * LANES] * qnw_ref[...]
            q_cols.append((_rope(qn_i, cos, sin_signed, lower_half) * (SWA_HEAD_DIM ** -0.5 * LOG2_E)).astype(BF16))
    kn = sk * lax.rsqrt(ssk * (1.0 / SWA_HEAD_DIM) + NORM_EPS) * knw_ref[...]
    kr = _rope(kn, cos, sin_signed, lower_half)

    upd = []
    for c in range(n_chunks):
        r0 = c * GLA_CHUNK
        ke_c = k_end_b[r0:r0 + GLA_CHUNK, :]
        k_stack = jnp.concatenate([ke_c * head_masks[h] for h in range(GLA_HEADS)], axis=0)
        v_stack = jnp.concatenate(
            [gv[r0:r0 + GLA_CHUNK, h * GLA_DV:(h + 1) * GLA_DV] for h in range(GLA_HEADS)], axis=0)
        upd.append(_dot(v_stack.T.astype(BF16), k_stack))

    group_masks = [_lane_group_mask(2 * LANES, SWA_HEAD_DIM, g).astype(BF16) for g in range(SWA_GROUP)]
    lane128 = lax.broadcasted_iota(jnp.int32, (1, LANES), 1)
    slot_j = lax.broadcasted_iota(jnp.int32, (SWA_BLOCK, SWA_GROUP * SWA_BLOCK), 0)
    qry_i = lax.broadcasted_iota(jnp.int32, (SWA_BLOCK, SWA_GROUP * SWA_BLOCK), 1) % SWA_BLOCK
    from_prev = slot_j > qry_i

    def kv_keys(k_blk):
        k_sw = pltpu.roll(k_blk, SWA_HEAD_DIM, axis=1)
        reps = []
        for kv in range(SWA_KV_HEADS):
            k_one = jnp.where((lane128 // SWA_HEAD_DIM) == kv, k_blk, k_sw).astype(BF16)
            reps.append(jnp.concatenate([k_one, k_one], axis=1))
        return reps

    keys = [kv_keys(k_prev[...])] + [kv_keys(kr[n * SWA_BLOCK:(n + 1) * SWA_BLOCK, :]) for n in range(n_blocks)]
    vt_f32 = [sv[n * SWA_BLOCK:(n + 1) * SWA_BLOCK, :].T for n in range(n_blocks)]
    vt = [v_prev[...].astype(BF16)] + [v.astype(BF16) for v in vt_f32]

    def swa_scores(n, kv):
        r0 = n * SWA_BLOCK
        q_kv = jnp.concatenate(q_cols[2 * kv:2 * kv + 2], axis=1)[r0:r0 + SWA_BLOCK, :]
        q_stack = jnp.concatenate([q_kv * group_masks[g] for g in range(SWA_GROUP)], axis=0)
        k_both = jnp.concatenate([keys[n][kv], keys[n + 1][kv]], axis=0)
        return _dot_nt(k_both, q_stack)

    def swa_softmax(n, kv, s_both):
        s_prev = s_both[0:SWA_BLOCK, :]
        if n == 0 and no_prev_bias is not None:
            s_prev = s_prev + no_prev_bias
        s_t = jnp.where(from_prev, s_prev, s_both[SWA_BLOCK:2 * SWA_BLOCK, :])
        sink = jnp.concatenate(
            [jnp.full((1, SWA_BLOCK), sinks[kv * SWA_GROUP + g], F32) for g in range(SWA_GROUP)], axis=1)
        m = jnp.maximum(jnp.max(s_t, axis=0, keepdims=True), sink)
        p_t = jnp.exp2(s_t - m)
        inv = 1.0 / (jnp.sum(p_t, axis=0, keepdims=True) + jnp.exp2(sink - m))
        p_both = jnp.concatenate([jnp.where(from_prev, p_t, 0.0), jnp.where(from_prev, 0.0, p_t)], axis=0)
        return p_both.astype(BF16), inv

    def swa_values(n, kv, p_both, inv):
        hd = slice(kv * SWA_HEAD_DIM, (kv + 1) * SWA_HEAD_DIM)
        vt_both = jnp.concatenate([vt[n][hd, :], vt[n + 1][hd, :]], axis=1)
        o_t = _dot(vt_both, p_both) * inv
        return [o_t[:, g * SWA_BLOCK:(g + 1) * SWA_BLOCK] for g in range(SWA_GROUP)]

    def gla_scores(h):
        att = _dot_nt(q_in_b * head_masks[h], k_in)
        return jnp.where(causal_chunk, att, 0.0).astype(BF16)

    def gla_values(h, att):
        return _dot(att, gvb[:, h * GLA_DV:(h + 1) * GLA_DV])

    steps = [(n, kv) for n in range(n_blocks) for kv in range(SWA_KV_HEADS)]
    assert len(steps) == 4 and GLA_HEADS == 4 and n_chunks == 4

    s0 = swa_scores(*steps[0])
    att0 = gla_scores(0)
    att1 = gla_scores(1)

    state = gla_state[...]
    o_inter = [[] for _ in range(GLA_HEADS)]
    for c in range(n_chunks):
        r0 = c * GLA_CHUNK
        q_c = q_in_b[r0:r0 + GLA_CHUNK, :]
        q_stack = jnp.concatenate([q_c * head_masks[h] for h in range(GLA_HEADS)], axis=0)
        oi = _dot_nt(q_stack, state.astype(BF16))
        for h in range(GLA_HEADS):
            o_inter[h].append(oi[h * GLA_CHUNK:(h + 1) * GLA_CHUNK, :])
        state = decay[r0:r0 + 1, :] * state + upd[c]
    gla_state[...] = state

    s1 = swa_scores(*steps[1])
    o_heads = swa_values(*steps[0], *swa_softmax(*steps[0], s0))
    ov0 = gla_values(0, att0)
    ov1 = gla_values(1, att1)
    att2 = gla_scores(2)
    att3 = gla_scores(3)
    yc_blocks = []
    s2 = swa_scores(*steps[2])
    o_heads += swa_values(*steps[1], *swa_softmax(*steps[1], s1))
    yc_blocks.append(jnp.concatenate(o_heads, axis=0).T)
    gate_slices(1)
    ov2 = gla_values(2, att2)
    ov3 = gla_values(3, att3)
    s3 = swa_scores(*steps[3])
    o_heads = swa_values(*steps[2], *swa_softmax(*steps[2], s2))
    gate_slices(2)
    o_heads += swa_values(*steps[3], *swa_softmax(*steps[3], s3))
    yc_blocks.append(jnp.concatenate(o_heads, axis=0).T)
    yc = jnp.concatenate(yc_blocks, axis=0).astype(BF16)
    k_prev[...] = kr[tok - SWA_BLOCK:tok, :]
    v_prev[...] = vt_f32[-1]

    br_a = _dot(ya, wbr_ref[0])
    gate_slices(3)

    yb_parts = []
    for h, ov in enumerate((ov0, ov1, ov2, ov3)):
        o_h = ov + jnp.concatenate(o_inter[h], axis=0)
        o_h = _rms_scale(o_h) * onw_ref[...]
        r_h = gr[:, h * GLA_DV:(h + 1) * GLA_DV]
        yb_parts.append((o_h * (r_h * jax.nn.sigmoid(r_h))).astype(BF16))
    yb = jnp.concatenate(yb_parts, axis=1)
    br_b = _dot(yb, wbr_ref[1])
    gate_slices(1)

    per_branch = D_MODEL // gate_cols
    assert len(gate_pieces) == N_BRANCHES * per_branch
    acc = x
    mixed_slices = []
    for j in range(per_branch):
        cols = slice(j * gate_cols, (j + 1) * gate_cols)
        br_c_j = _dot(yc, wbr_ref[2, :, cols])
        mixed_j = (gate_pieces[j] * br_a[:, cols] + gate_pieces[per_branch + j] * br_b[:, cols]
                   + gate_pieces[2 * per_branch + j] * br_c_j)
        mixed_slices.append(mixed_j.astype(BF16))
        if j >= 1:
            acc = acc + _dot(mixed_slices[j - 1], wo_ref[(j - 1) * gate_cols:j * gate_cols, :])
    return acc + _dot(mixed_slices[-1], wo_ref[(per_branch - 1) * gate_cols:per_branch * gate_cols, :])


def _ffn_kernel(x_ref, n2w_ref, wup_ref, wdown_ref, *rest):
    if len(rest) == 1:
        (out_ref,) = rest
    else:
        out_ref = rest[3]
        _cast_mixer_rows(*rest[:3], *rest[4:])
    x = x_ref[...]
    hb = (_rms_scale(x) * n2w_ref[...]).astype(BF16)
    acc = x
    for c in range(D_FF // FF_CHUNK):
        a = jnp.maximum(_dot(hb, wup_ref[:, c * FF_CHUNK:(c + 1) * FF_CHUNK]), 0.0)
        acc = acc + _dot((a * a).astype(BF16), wdown_ref[c * FF_CHUNK:(c + 1) * FF_CHUNK, :])
    out_ref[...] = acc


def _const_spec(shape):
    zeros = (0,) * len(shape)
    return pl.BlockSpec(shape, lambda *_: zeros, pipeline_mode=pl.Buffered(1))


def _row_block(n_rows, n_steps, dtype):
    rows = n_rows // n_steps
    assert rows * n_steps == n_rows and rows % (2 * SUBLANES if dtype == BF16 else SUBLANES) == 0
    return rows


def _mixer_call(x, cos, sin, p, w_up_f32, w_down_f32, layer, tok):
    bsz, seq, d = x.shape
    n_t = seq // tok
    grid = (bsz, n_t)
    n_steps = bsz * n_t
    tile = lambda b, t: (b, t, 0)
    step_rows = lambda b, t: (b * n_t + t, 0)
    layer_rows = lambda b, t: (layer, b * n_t + t, 0)
    up_rows = _row_block(d, n_steps, BF16)
    down_rows = _row_block(D_FF, n_steps, BF16)
    spec = _const_spec
    in_specs = [
        pl.BlockSpec(memory_space=pltpu.SMEM),
        pl.BlockSpec((1, tok, d), tile),
        pl.BlockSpec((1, tok, LANES), tile),
        pl.BlockSpec((1, tok, LANES), tile),
        spec((1, d)),
        spec((d, IN_A)),
        spec((d, GA_PAD)),
        spec((d, IN_B)),
        spec((CONV_WIDTH, CONV_DIM)),
        spec((1, CONV_DIM)),
        spec((GA_PAD, GLA_HEADS * GLA_DK)),
        spec((1, GLA_HEADS * GLA_DK)),
        spec((1, GLA_DV)),
        spec((1, LANES)),
        spec((1, LANES)),
        spec((N_BRANCHES, BRANCH_WIDTH, d)),
        spec((d, d)),
        pl.BlockSpec((None, up_rows, D_FF), layer_rows),
        pl.BlockSpec((None, down_rows, d), layer_rows),
    ]
    out_specs = [
        pl.BlockSpec((1, tok, d), tile),
        pl.BlockSpec((up_rows, D_FF), step_rows),
        pl.BlockSpec((down_rows, d), step_rows),
    ]
    out_shape = [
        jax.ShapeDtypeStruct(x.shape, F32),
        jax.ShapeDtypeStruct((d, D_FF), BF16),
        jax.ShapeDtypeStruct((D_FF, d), BF16),
    ]
    scratch = [
        pltpu.VMEM((SUB_TILE + SUBLANES, CONV_DIM), F32),
        pltpu.VMEM((GLA_DV, GLA_HEADS * GLA_DK), F32),
        pltpu.VMEM((SWA_BLOCK, SWA_KV_HEADS * SWA_HEAD_DIM), F32),
        pltpu.VMEM((SWA_BLOCK, SWA_KV_HEADS * SWA_HEAD_DIM), F32),
    ]
    out, w_up_b16, w_down_b16 = pl.pallas_call(
        functools.partial(_mixer_kernel, tok=tok),
        grid=grid,
        in_specs=in_specs,
        out_specs=out_specs,
        out_shape=out_shape,
        scratch_shapes=scratch,
        compiler_params=pltpu.CompilerParams(
            dimension_semantics=("arbitrary", "arbitrary"),
            vmem_limit_bytes=V7X_VMEM_BYTES * 3 // 4),
        name="mixer",
    )(p["sinks"], x, cos, sin, p["n1w"], p["w_in_a"], p["w_in_ga"], p["w_in_b"], p["conv_w"], p["conv_b"],
      p["wa2"], p["ba"], p["onw"], p["qnw"], p["knw"], p["w_branch"], p["w_o"], w_up_f32, w_down_f32)
    return out, w_up_b16, w_down_b16


def _mixer_cast_specs(d, layer, block_of):
    n_blocks = d // CAST_ROWS
    br_rows = N_BRANCHES * BRANCH_WIDTH // n_blocks
    assert n_blocks * CAST_ROWS == d and br_rows % (2 * SUBLANES) == 0
    rows_in = lambda *ids: (layer, block_of(*ids), 0)
    cols_in = lambda *ids: (layer, 0, block_of(*ids))
    rows_out = lambda *ids: (block_of(*ids), 0)
    in_specs = [
        pl.BlockSpec((None, N_IN, CAST_ROWS), cols_in),
        pl.BlockSpec((None, br_rows, d), rows_in),
        pl.BlockSpec((None, CAST_ROWS, d), rows_in),
    ]
    out_specs = [
        pl.BlockSpec((CAST_ROWS, IN_A), rows_out),
        pl.BlockSpec((CAST_ROWS, GA_PAD), rows_out),
        pl.BlockSpec((CAST_ROWS, IN_B), rows_out),
        pl.BlockSpec((br_rows, d), rows_out),
        pl.BlockSpec((CAST_ROWS, d), rows_out),
    ]
    out_shape = [
        jax.ShapeDtypeStruct((d, IN_A), BF16),
        jax.ShapeDtypeStruct((d, GA_PAD), BF16),
        jax.ShapeDtypeStruct((d, IN_B), BF16),
        jax.ShapeDtypeStruct((N_BRANCHES * BRANCH_WIDTH, d), BF16),
        jax.ShapeDtypeStruct((d, d), BF16),
    ]
    return in_specs, out_specs, out_shape


def _mixer_cast_operands(w_in, w_branch, w_o):
    d = w_o.shape[-1]
    return [jnp.swapaxes(w_in, 1, 2), w_branch.reshape(-1, N_BRANCHES * BRANCH_WIDTH, d), w_o]


def _mixer_weight_dict(w_in_a, w_in_ga, w_in_b, w_branch, w_o):
    d = w_o.shape[0]
    return {"w_in_a": w_in_a, "w_in_ga": w_in_ga, "w_in_b": w_in_b,
            "w_branch": w_branch.reshape(N_BRANCHES, BRANCH_WIDTH, d), "w_o": w_o}


def _ffn_call(x2, n2w, w_up, w_down, tile, next_mixer_f32=None, next_layer=None):
    n, d = x2.shape
    n_steps = n // tile
    step_rows = lambda i: (i, 0)
    in_specs = [
        pl.BlockSpec((tile, d), step_rows),
        _const_spec((1, d)),
        _const_spec((d, D_FF)),
        _const_spec((D_FF, d)),
    ]
    out_specs = [pl.BlockSpec((tile, d), step_rows)]
    out_shape = [jax.ShapeDtypeStruct(x2.shape, F32)]
    operands = [x2, n2w, w_up, w_down]
    if next_mixer_f32 is not None:
        w_in_f32, w_branch_f32, w_o_f32 = next_mixer_f32
        n_blocks = d // CAST_ROWS
        assert n_steps % n_blocks == 0
        cast_in, cast_out, cast_shape = _mixer_cast_specs(d, next_layer, lambda i: i // (n_steps // n_blocks))
        in_specs += cast_in
        out_specs += cast_out
        out_shape += cast_shape
        operands += _mixer_cast_operands(w_in_f32, w_branch_f32, w_o_f32)
    outs = pl.pallas_call(
        _ffn_kernel,
        grid=(n_steps,),
        in_specs=in_specs,
        out_specs=out_specs,
        out_shape=out_shape,
        compiler_params=pltpu.CompilerParams(
            dimension_semantics=("arbitrary",),
            vmem_limit_bytes=V7X_VMEM_BYTES * 7 // 8),
        name="ffn",
    )(*operands)
    if next_mixer_f32 is None:
        return outs[0], None
    return outs[0], _mixer_weight_dict(*outs[1:])


def _rope_spread_matrix():
    half = ROPE_DIM // 2
    c = np.arange(LANES)[:, None]
    l = np.arange(LANES)[None, :]
    hit = ((c % half) == (l % half)) & ((l % SWA_HEAD_DIM) < ROPE_DIM)
    sign = np.where((l % SWA_HEAD_DIM) < half, -1.0, 1.0)
    m = np.zeros((2 * LANES, 2 * LANES), np.float32)
    m[:LANES, :LANES] = hit
    m[LANES:, LANES:] = hit * sign
    return jnp.asarray(m, BF16)


def _rope_tables_and_first_weights(positions, w_in, w_branch, w_o):
    bsz, seq = positions.shape
    d = w_o.shape[-1]
    n = bsz * seq
    n_steps = d // CAST_ROWS
    tokens = n // n_steps
    assert tokens * n_steps == n and tokens % (ROPE_TOKENS_PER_ROW * SUBLANES) == 0
    half = ROPE_DIM // 2
    inv_freq = ROPE_THETA ** (-jnp.arange(0, ROPE_DIM, 2, dtype=F32) / ROPE_DIM)
    invf_dense = jnp.tile(inv_freq, LANES // half)[None, :]
    pos_dense = jnp.repeat(positions.astype(jnp.int32).reshape(n), half).reshape(n // ROPE_TOKENS_PER_ROW, LANES)
    rows = tokens // ROPE_TOKENS_PER_ROW
    cast_in, cast_out, cast_shape = _mixer_cast_specs(d, 0, lambda i: i)
    outs = pl.pallas_call(
        functools.partial(_rope_table_kernel, tokens=tokens),
        grid=(n_steps,),
        in_specs=[
            pl.BlockSpec((rows, LANES), lambda i: (i, 0)),
            _const_spec((1, LANES)),
            _const_spec((2 * LANES, 2 * LANES)),
        ] + cast_in,
        out_specs=[pl.BlockSpec((tokens, LANES), lambda i: (i, 0))] * 2 + cast_out,
        out_shape=[jax.ShapeDtypeStruct((n, LANES), F32)] * 2 + cast_shape,
        compiler_params=pltpu.CompilerParams(
            dimension_semantics=("arbitrary",),
            vmem_limit_bytes=V7X_VMEM_BYTES * 3 // 4),
        name="rope_tables",
    )(pos_dense, invf_dense, _rope_spread_matrix(), *_mixer_cast_operands(w_in, w_branch, w_o))
    cos, sin = outs[0], outs[1]
    return cos.reshape(bsz, seq, LANES), sin.reshape(bsz, seq, LANES), _mixer_weight_dict(*outs[2:])


def _small_mixer_params(layer, norm1_w, conv_w, conv_b, gla_wa2, gla_ba, gla_onorm_w, q_norm_w, k_norm_w, sinks):
    heads_per_tile = LANES // SWA_HEAD_DIM
    return {
        "sinks": sinks[layer].astype(F32),
        "n1w": norm1_w[layer][None, :],
        "conv_w": conv_w[layer],
        "conv_b": conv_b[layer][None, :],
        "wa2": jnp.pad(gla_wa2[layer], ((0, GA_PAD - GLA_RANK), (0, 0))).astype(BF16),
        "ba": gla_ba[layer][None, :],
        "onw": gla_onorm_w[layer][None, :],
        "qnw": jnp.tile(q_norm_w[layer], heads_per_tile)[None, :],
        "knw": jnp.tile(k_norm_w[layer], heads_per_tile)[None, :],
    }


def kernel(x, positions, norm1_w, w_in, conv_w, conv_b, gla_wa2, gla_ba, gla_onorm_w, q_norm_w, k_norm_w,
           sinks, w_branch, w_o, norm2_w, w_up, w_down):
    bsz, seq, d = x.shape
    depth = w_in.shape[0]
    assert d == D_MODEL and w_in.shape[2] == N_IN
    tok = min(TOK_TILE, seq)
    assert seq % tok == 0 and tok % SUB_TILE == 0 and SUB_TILE % (2 * SWA_BLOCK) == 0
    ffn_tile = min(FFN_TILE, bsz * seq)
    assert (bsz * seq) % ffn_tile == 0
    cos, sin, mixer_w = _rope_tables_and_first_weights(positions, w_in, w_branch, w_o)
    for layer in range(depth):
        p = dict(mixer_w, **_small_mixer_params(layer, norm1_w, conv_w, conv_b, gla_wa2, gla_ba, gla_onorm_w,
                                                q_norm_w, k_norm_w, sinks))
        x, w_up_b16, w_down_b16 = _mixer_call(x, cos, sin, p, w_up, w_down, layer, tok)
        nxt = (w_in, w_branch, w_o) if layer + 1 < depth else None
        x2, mixer_w = _ffn_call(x.reshape(bsz * seq, d), norm2_w[layer][None, :], w_up_b16, w_down_b16, ffn_tile,
                                nxt, layer + 1)
        x = x2.reshape(bsz, seq, d)
    return x
```

```python
import functools

import jax
import jax.numpy as jnp
import numpy as np
from jax import lax
from jax.experimental import pallas as pl
from jax.experimental.pallas import tpu as pltpu

F32 = jnp.float32
BF16 = jnp.bfloat16

D_MODEL = 1024
CONV_DIM = 512
CONV_WIDTH = 3
GLA_HEADS = 4
GLA_DK = 64
GLA_DV = 128
GLA_RANK = 16
GLA_TAU = 16.0
GLA_CHUNK = 64
SWA_HEADS = 8
SWA_KV_HEADS = 2
SWA_GROUP = SWA_HEADS // SWA_KV_HEADS
SWA_HEAD_DIM = 64
SWA_BLOCK = 128
ROPE_THETA = 500000.0
ROPE_DIM = SWA_HEAD_DIM // 4
N_BRANCHES = 3
BRANCH_WIDTH = 512
D_FF = 4 * D_MODEL
NORM_EPS = 1e-6
LOG2_E = 1.4426950408889634

LANES = 128
SUBLANES = 8
V7X_VMEM_BYTES = 64 * 1024 * 1024

GA_PAD = LANES
IN_A = 3 * CONV_DIM + 2 * GLA_HEADS * GLA_DK + 2 * GLA_HEADS * GLA_DV
IN_B = SWA_HEADS * SWA_HEAD_DIM + 2 * SWA_KV_HEADS * SWA_HEAD_DIM + N_BRANCHES * D_MODEL
N_IN = IN_A + GLA_RANK + IN_B
A_CONV = 0
A_GQ = A_CONV + 3 * CONV_DIM
A_GK = A_GQ + GLA_HEADS * GLA_DK
A_GV = A_GK + GLA_HEADS * GLA_DK
A_GR = A_GV + GLA_HEADS * GLA_DV
B_SQ = 0
B_SK = B_SQ + SWA_HEADS * SWA_HEAD_DIM
B_SV = B_SK + SWA_KV_HEADS * SWA_HEAD_DIM
B_GATE = B_SV + SWA_KV_HEADS * SWA_HEAD_DIM

ROPE_TOKENS_PER_ROW = LANES // (ROPE_DIM // 2)
SUB_TILE = 256
TOK_TILE = 512
FFN_TILE = 1024
FF_CHUNK = 1024
CAST_ROWS = 128


def _dot(a, b):
    return jnp.dot(a, b, preferred_element_type=F32)


def _dot_nt(a, b):
    return lax.dot_general(a, b, (((1,), (1,)), ((), ())), preferred_element_type=F32)


def _split_dot(m_bf16, v_f32):
    hi = v_f32.astype(BF16)
    lo = (v_f32 - hi.astype(F32)).astype(BF16)
    return _dot(m_bf16, hi) + _dot(m_bf16, lo)


def _split_dot_rhs(v_f32, m_bf16):
    hi = v_f32.astype(BF16)
    lo = (v_f32 - hi.astype(F32)).astype(BF16)
    return _dot(hi, m_bf16) + _dot(lo, m_bf16)


def _rms_scale(x):
    return x * lax.rsqrt(jnp.mean(x * x, axis=-1, keepdims=True) + NORM_EPS)


def _lane_group_mask(width, group, index):
    lane = lax.broadcasted_iota(jnp.int32, (1, width), 1)
    return jnp.where((lane // group) == index, 1.0, 0.0).astype(F32)


def _split3(v_f32):
    hi = v_f32.astype(BF16)
    r1 = v_f32 - hi.astype(F32)
    mid = r1.astype(BF16)
    lo = (r1 - mid.astype(F32)).astype(BF16)
    return hi, mid, lo


def _cast_mixer_rows(win_t_ref, wbr_f32_ref, wo_f32_ref, wina_ref, winga_ref, winb_ref, wbr_ref, wo_ref):
    wina_ref[...] = win_t_ref[0:IN_A, :].T.astype(BF16)
    lane = lax.broadcasted_iota(jnp.int32, (CAST_ROWS, GA_PAD), 1)
    winga_ref[...] = jnp.where(lane < GLA_RANK, win_t_ref[IN_A:IN_A + GA_PAD, :].T, 0.0).astype(BF16)
    winb_ref[...] = win_t_ref[IN_A + GLA_RANK:N_IN, :].T.astype(BF16)
    wbr_ref[...] = wbr_f32_ref[...].astype(BF16)
    wo_ref[...] = wo_f32_ref[...].astype(BF16)


def _rope_table_kernel(pos_ref, invf_ref, spread_ref, *rest, tokens):
    cos_ref, sin_ref = rest[3], rest[4]
    _cast_mixer_rows(*rest[:3], *rest[5:])
    rows = tokens // ROPE_TOKENS_PER_ROW
    ang = pos_ref[...].astype(F32) * invf_ref[...]
    dense = jnp.concatenate([jnp.cos(ang), jnp.sin(ang)], axis=1)
    by_tok = jnp.broadcast_to(dense[:, None, :], (rows, ROPE_TOKENS_PER_ROW, 2 * LANES)).reshape(tokens, 2 * LANES)
    tok_c = lax.broadcasted_iota(jnp.int32, (tokens, 2 * LANES), 0) % ROPE_TOKENS_PER_ROW
    slot_c = (lax.broadcasted_iota(jnp.int32, (tokens, 2 * LANES), 1) % LANES) // (ROPE_DIM // 2)
    own = jnp.where(tok_c == slot_c, by_tok, 0.0)
    out = sum(_dot(piece, spread_ref[...]) for piece in _split3(own))
    lane_in_head = lax.broadcasted_iota(jnp.int32, (1, LANES), 1) % SWA_HEAD_DIM
    cos_ref[...] = out[:, 0:LANES] + jnp.where(lane_in_head >= ROPE_DIM, 1.0, 0.0)
    sin_ref[...] = out[:, LANES:2 * LANES]


def _rope(x, cos, sin_signed, lower_half):
    half = ROPE_DIM // 2
    up = pltpu.roll(x, LANES - half, axis=1)
    down = pltpu.roll(x, half, axis=1)
    return x * cos + jnp.where(lower_half, up, down) * sin_signed


def _mixer_kernel(sinks_ref, x_ref, cos_ref, sin_ref, n1w_ref, wina_ref, winga_ref, winb_ref, convw_ref, convb_ref,
                  wa2_ref, ba_ref, onw_ref, qnw_ref, knw_ref, wbr_ref, wo_ref, wup_f32_ref, wdown_f32_ref,
                  out_ref, wup_ref, wdown_ref, conv_buf, gla_state, k_prev, v_prev, *, tok):
    t = pl.program_id(1)

    @pl.when(t == 0)
    def _():
        conv_buf[0:SUBLANES, :] = jnp.zeros((SUBLANES, CONV_DIM), F32)
        gla_state[...] = jnp.zeros_like(gla_state)
        k_prev[...] = jnp.zeros_like(k_prev)
        v_prev[...] = jnp.zeros_like(v_prev)

    weights = (n1w_ref, wina_ref, winga_ref, winb_ref, convw_ref, convb_ref, wa2_ref, ba_ref, onw_ref, qnw_ref,
               knw_ref, wbr_ref, wo_ref)
    state = (conv_buf, gla_state, k_prev, v_prev)
    sinks = [sinks_ref[h] * LOG2_E for h in range(SWA_HEADS)]
    for s in range(tok // SUB_TILE):
        rows = slice(s * SUB_TILE, (s + 1) * SUB_TILE)
        no_prev_bias = jnp.where(t == 0, -jnp.inf, 0.0) if s == 0 else None
        out_ref[0, rows, :] = _mixer_subtile(x_ref[0, rows, :], cos_ref[0, rows, :], sin_ref[0, rows, :],
                                             no_prev_bias, sinks, weights, state)
    wup_ref[...] = wup_f32_ref[...].astype(BF16)
    wdown_ref[...] = wdown_f32_ref[...].astype(BF16)


def _mixer_subtile(x, cos, sin_signed, no_prev_bias, sinks, weights, state):
    (n1w_ref, wina_ref, winga_ref, winb_ref, convw_ref, convb_ref, wa2_ref, ba_ref, onw_ref, qnw_ref,
     knw_ref, wbr_ref, wo_ref) = weights
    conv_buf, gla_state, k_prev, v_prev = state
    tok = SUB_TILE
    n_chunks = tok // GLA_CHUNK
    n_blocks = tok // SWA_BLOCK

    ub = (_rms_scale(x) * n1w_ref[...]).astype(BF16)

    gate_cols = 2 * LANES
    gate_pieces = []

    def gate_slices(count):
        for _ in range(count):
            c0 = B_GATE + len(gate_pieces) * gate_cols
            gate_pieces.append(jax.nn.sigmoid(_dot(ub, winb_ref[:, c0:c0 + gate_cols])))

    ga = _dot(ub, winga_ref[...])
    pg = _dot(ub, wina_ref[:, A_GQ:IN_A])
    gq = pg[:, 0:A_GK - A_GQ] * (GLA_DK ** -0.5)
    gk = pg[:, A_GK - A_GQ:A_GV - A_GQ]
    gv = pg[:, A_GV - A_GQ:A_GR - A_GQ]
    gr = pg[:, A_GR - A_GQ:IN_A - A_GQ]
    ps = _dot(ub, winb_ref[:, B_SQ:B_GATE])
    sq = ps[:, B_SQ:B_SK]
    sk = ps[:, B_SK:B_SV]
    sv = ps[:, B_SV:B_GATE]

    z = _dot(ga.astype(BF16), wa2_ref[...]) + ba_ref[...]
    glog = (jnp.minimum(z, 0.0) - jnp.log(1.0 + jnp.exp(-jnp.abs(z)))) * (1.0 / GLA_TAU)
    gate_slices(1)

    gi = lax.broadcasted_iota(jnp.int32, (2 * LANES, 2 * LANES), 0) // SWA_HEAD_DIM
    gj = lax.broadcasted_iota(jnp.int32, (2 * LANES, 2 * LANES), 1) // SWA_HEAD_DIM
    head_ones = jnp.where(gi == gj, 1.0, 0.0).astype(BF16)
    n_qcols = SWA_HEADS * SWA_HEAD_DIM // (2 * LANES)
    q_halves = [sq[:, j * 2 * LANES:(j + 1) * 2 * LANES] for j in range(n_qcols)]
    ssq = [_dot((qj * qj).astype(BF16), head_ones) for qj in q_halves]
    ssk = _dot((sk * sk).astype(BF16), head_ones[0:LANES, 0:LANES])

    cx = _dot(ub, wina_ref[:, A_CONV:A_CONV + CONV_DIM])
    cc = _dot(ub, wina_ref[:, A_CONV + 2 * CONV_DIM:A_CONV + 3 * CONV_DIM])
    gate_slices(2)
    cb = _dot(ub, wina_ref[:, A_CONV + CONV_DIM:A_CONV + 2 * CONV_DIM])
    uc = cc * cx
    conv_buf[SUBLANES:SUBLANES + tok, :] = uc
    u1 = conv_buf[SUBLANES - 1:SUBLANES - 1 + tok, :]
    u2 = conv_buf[SUBLANES - 2:SUBLANES - 2 + tok, :]
    yconv = convw_ref[0:1, :] * u2 + convw_ref[1:2, :] * u1 + convw_ref[2:3, :] * uc
    ya = (cb * (yconv + convb_ref[...])).astype(BF16)
    conv_buf[0:SUBLANES, :] = uc[tok - SUBLANES:tok, :]

    ri = lax.broadcasted_iota(jnp.int32, (tok, tok), 0)
    ci = lax.broadcasted_iota(jnp.int32, (tok, tok), 1)
    same_chunk = (ri // GLA_CHUNK) == (ci // GLA_CHUNK)
    causal_chunk = same_chunk & (ci <= ri)
    l_tri = jnp.where(causal_chunk, 1.0, 0.0).astype(BF16)
    b = _split_dot(l_tri, glog)
    b_last = jnp.concatenate(
        [jnp.broadcast_to(b[(c + 1) * GLA_CHUNK - 1:(c + 1) * GLA_CHUNK, :], (GLA_CHUNK, GLA_HEADS * GLA_DK))
         for c in range(n_chunks)], axis=0)
    gate_slices(2)
    q_in_b = (gq * jnp.exp(b)).astype(BF16)
    k_in = (gk * jnp.exp(-b)).astype(BF16)
    k_end_b = (gk * jnp.exp(b_last - b)).astype(BF16)
    decay = jnp.exp(b_last)
    head_masks = [_lane_group_mask(GLA_HEADS * GLA_DK, GLA_DK, h).astype(BF16) for h in range(GLA_HEADS)]
    gvb = gv.astype(BF16)

    lower_half = (lax.broadcasted_iota(jnp.int32, (1, LANES), 1) % SWA_HEAD_DIM) < ROPE_DIM // 2
    q_cols = []
    for j in range(n_qcols):
        qn = q_halves[j] * lax.rsqrt(ssq[j] * (1.0 / SWA_HEAD_DIM) + NORM_EPS)
        for i in range(2):
            qn_i = qn[:, i * LANES:(i + 1) * LANES] * qnw_ref[...]
            q_cols.append((_rope(qn_i, cos, sin_signed, lower_half) * (SWA_HEAD_DIM ** -0.5 * LOG2_E)).astype(BF16))
    kn = sk * lax.rsqrt(ssk * (1.0 / SWA_HEAD_DIM) + NORM_EPS) * knw_ref[...]
    kr = _rope(kn, cos, sin_signed, lower_half)

    upd = []
    for c in range(n_chunks):
        r0 = c * GLA_CHUNK
        ke_c = k_end_b[r0:r0 + GLA_CHUNK, :]
        k_stack = jnp.concatenate([ke_c * head_masks[h] for h in range(GLA_HEADS)], axis=0)
        v_stack = jnp.concatenate(
            [gv[r0:r0 + GLA_CHUNK, h * GLA_DV:(h + 1) * GLA_DV] for h in range(GLA_HEADS)], axis=0)
        upd.append(_dot(v_stack.T.astype(BF16), k_stack))

    group_masks = [_lane_group_mask(2 * LANES, SWA_HEAD_DIM, g).astype(BF16) for g in range(SWA_GROUP)]
    lane128 = lax.broadcasted_iota(jnp.int32, (1, LANES), 1)
    slot_j = lax.broadcasted_iota(jnp.int32, (SWA_BLOCK, SWA_GROUP * SWA_BLOCK), 0)
    qry_i = lax.broadcasted_iota(jnp.int32, (SWA_BLOCK, SWA_GROUP * SWA_BLOCK), 1) % SWA_BLOCK
    from_prev = slot_j > qry_i

    def kv_keys(k_blk):
        k_sw = pltpu.roll(k_blk, SWA_HEAD_DIM, axis=1)
        reps = []
        for kv in range(SWA_KV_HEADS):
            k_one = jnp.where((lane128 // SWA_HEAD_DIM) == kv, k_blk, k_sw).astype(BF16)
            reps.append(jnp.concatenate([k_one, k_one], axis=1))
        return reps

    keys = [kv_keys(k_prev[...])] + [kv_keys(kr[n * SWA_BLOCK:(n + 1) * SWA_BLOCK, :]) for n in range(n_blocks)]
    vt_f32 = [sv[n * SWA_BLOCK:(n + 1) * SWA_BLOCK, :].T for n in range(n_blocks)]
    vt = [v_prev[...].astype(BF16)] + [v.astype(BF16) for v in vt_f32]

    def swa_scores(n, kv):
        r0 = n * SWA_BLOCK
        q_kv = jnp.concatenate(q_cols[2 * kv:2 * kv + 2], axis=1)[r0:r0 + SWA_BLOCK, :]
        q_stack = jnp.concatenate([q_kv * group_masks[g] for g in range(SWA_GROUP)], axis=0)
        k_both = jnp.concatenate([keys[n][kv], keys[n + 1][kv]], axis=0)
        return _dot_nt(k_both, q_stack)

    def swa_softmax(n, kv, s_both):
        s_prev = s_both[0:SWA_BLOCK, :]
        if n == 0 and no_prev_bias is not None:
            s_prev = s_prev + no_prev_bias
        s_t = jnp.where(from_prev, s_prev, s_both[SWA_BLOCK:2 * SWA_BLOCK, :])
        sink = jnp.concatenate(
            [jnp.full((1, SWA_BLOCK), sinks[kv * SWA_GROUP + g], F32) for g in range(SWA_GROUP)], axis=1)
        m = jnp.maximum(jnp.max(s_t, axis=0, keepdims=True), sink)
        p_t = jnp.exp2(s_t - m)
        inv = 1.0 / (jnp.sum(p_t, axis=0, keepdims=True) + jnp.exp2(sink - m))
        p_both = jnp.concatenate([jnp.where(from_prev, p_t, 0.0), jnp.where(from_prev, 0.0, p_t)], axis=0)
        return p_both.astype(BF16), inv

    def swa_values(n, kv, p_both, inv):
        hd = slice(kv * SWA_HEAD_DIM, (kv + 1) * SWA_HEAD_DIM)
        vt_both = jnp.concatenate([vt[n][hd, :], vt[n + 1][hd, :]], axis=1)
        o_t = _dot(vt_both, p_both) * inv
        return [o_t[:, g * SWA_BLOCK:(g + 1) * SWA_BLOCK] for g in range(SWA_GROUP)]

    def gla_scores(h):
        att = _dot_nt(q_in_b * head_masks[h], k_in)
        return jnp.where(causal_chunk, att, 0.0).astype(BF16)

    def gla_values(h, att):
        return _dot(att, gvb[:, h * GLA_DV:(h + 1) * GLA_DV])

    steps = [(n, kv) for n in range(n_blocks) for kv in range(SWA_KV_HEADS)]
    assert len(steps) == 4 and GLA_HEADS == 4 and n_chunks == 4

    s0 = swa_scores(*steps[0])
    att0 = gla_scores(0)
    att1 = gla_scores(1)

    state = gla_state[...]
    o_inter = [[] for _ in range(GLA_HEADS)]
    for c in range(n_chunks):
        r0 = c * GLA_CHUNK
        q_c = q_in_b[r0:r0 + GLA_CHUNK, :]
        q_stack = jnp.concatenate([q_c * head_masks[h] for h in range(GLA_HEADS)], axis=0)
        oi = _dot_nt(q_stack, state.astype(BF16))
        for h in range(GLA_HEADS):
            o_inter[h].append(oi[h * GLA_CHUNK:(h + 1) * GLA_CHUNK, :])
        state = decay[r0:r0 + 1, :] * state + upd[c]
    gla_state[...] = state

    s1 = swa_scores(*steps[1])
    o_heads = swa_values(*steps[0], *swa_softmax(*steps[0], s0))
    ov0 = gla_values(0, att0)
    ov1 = gla_values(1, att1)
    att2 = gla_scores(2)
    att3 = gla_scores(3)
    yc_blocks = []
    s2 = swa_scores(*steps[2])
    o_heads += swa_values(*steps[1], *swa_softmax(*steps[1], s1))
    yc_blocks.append(jnp.concatenate(o_heads, axis=0).T)
    gate_slices(1)
    ov2 = gla_values(2, att2)
    ov3 = gla_values(3, att3)
    s3 = swa_scores(*steps[3])
    o_heads = swa_values(*steps[2], *swa_softmax(*steps[2], s2))
    gate_slices(2)
    o_heads += swa_values(*steps[3], *swa_softmax(*steps[3], s3))
    yc_blocks.append(jnp.concatenate(o_heads, axis=0).T)
    yc = jnp.concatenate(yc_blocks, axis=0).astype(BF16)
    k_prev[...] = kr[tok - SWA_BLOCK:tok, :]
    v_prev[...] = vt_f32[-1]

    br_a = _dot(ya, wbr_ref[0])
    gate_slices(3)

    yb_parts = []
    for h, ov in enumerate((ov0, ov1, ov2, ov3)):
        o_h = ov + jnp.concatenate(o_inter[h], axis=0)
        o_h = _rms_scale(o_h) * onw_ref[...]
        r_h = gr[:, h * GLA_DV:(h + 1) * GLA_DV]
        yb_parts.append((o_h * (r_h * jax.nn.sigmoid(r_h))).astype(BF16))
    yb = jnp.concatenate(yb_parts, axis=1)
    br_b = _dot(yb, wbr_ref[1])
    gate_slices(1)

    per_branch = D_MODEL // gate_cols
    assert len(gate_pieces) == N_BRANCHES * per_branch
    acc = x
    mixed_slices = []
    for j in range(per_branch):
        cols = slice(j * gate_cols, (j + 1) * gate_cols)
        br_c_j = _dot(yc, wbr_ref[2, :, cols])
        mixed_j = (gate_pieces[j] * br_a[:, cols] + gate_pieces[per_branch + j] * br_b[:, cols]
                   + gate_pieces[2 * per_branch + j] * br_c_j)
        mixed_slices.append(mixed_j.astype(BF16))
        if j >= 1:
            acc = acc + _dot(mixed_slices[j - 1], wo_ref[(j - 1) * gate_cols:j * gate_cols, :])
    return acc + _dot(mixed_slices[-1], wo_ref[(per_branch - 1) * gate_cols:per_branch * gate_cols, :])


def _ffn_kernel(x_ref, n2w_ref, wup_ref, wdown_ref, *rest):
    if len(rest) == 1:
        (out_ref,) = rest
    else:
        out_ref = rest[3]
        _cast_mixer_rows(*rest[:3], *rest[4:])
    half = x_ref.shape[0] // 2
    for s in range(2):
        rows = slice(s * half, (s + 1) * half)
        x = x_ref[rows, :]
        hb = (_rms_scale(x) * n2w_ref[...]).astype(BF16)
        acc = x
        for c in range(D_FF // FF_CHUNK):
            a = jnp.maximum(_dot(hb, wup_ref[:, c * FF_CHUNK:(c + 1) * FF_CHUNK]), 0.0)
            acc = acc + _dot((a * a).astype(BF16), wdown_ref[c * FF_CHUNK:(c + 1) * FF_CHUNK, :])
        out_ref[rows, :] = acc


def _const_spec(shape):
    zeros = (0,) * len(shape)
    return pl.BlockSpec(shape, lambda *_: zeros, pipeline_mode=pl.Buffered(1))


def _row_block(n_rows, n_steps, dtype):
    rows = n_rows // n_steps
    assert rows * n_steps == n_rows and rows % (2 * SUBLANES if dtype == BF16 else SUBLANES) == 0
    return rows


def _mixer_call(x, cos, sin, p, w_up_f32, w_down_f32, layer, tok):
    bsz, seq, d = x.shape
    n_t = seq // tok
    grid = (bsz, n_t)
    n_steps = bsz * n_t
    tile = lambda b, t: (b, t, 0)
    step_rows = lambda b, t: (b * n_t + t, 0)
    layer_rows = lambda b, t: (layer, b * n_t + t, 0)
    up_rows = _row_block(d, n_steps, BF16)
    down_rows = _row_block(D_FF, n_steps, BF16)
    spec = _const_spec
    in_specs = [
        pl.BlockSpec(memory_space=pltpu.SMEM),
        pl.BlockSpec((1, tok, d), tile),
        pl.BlockSpec((1, tok, LANES), tile),
        pl.BlockSpec((1, tok, LANES), tile),
        spec((1, d)),
        spec((d, IN_A)),
        spec((d, GA_PAD)),
        spec((d, IN_B)),
        spec((CONV_WIDTH, CONV_DIM)),
        spec((1, CONV_DIM)),
        spec((GA_PAD, GLA_HEADS * GLA_DK)),
        spec((1, GLA_HEADS * GLA_DK)),
        spec((1, GLA_DV)),
        spec((1, LANES)),
        spec((1, LANES)),
        spec((N_BRANCHES, BRANCH_WIDTH, d)),
        spec((d, d)),
        pl.BlockSpec((None, up_rows, D_FF), layer_rows),
        pl.BlockSpec((None, down_rows, d), layer_rows),
    ]
    out_specs = [
        pl.BlockSpec((1, tok, d), tile),
        pl.BlockSpec((up_rows, D_FF), step_rows),
        pl.BlockSpec((down_rows, d), step_rows),
    ]
    out_shape = [
        jax.ShapeDtypeStruct(x.shape, F32),
        jax.ShapeDtypeStruct((d, D_FF), BF16),
        jax.ShapeDtypeStruct((D_FF, d), BF16),
    ]
    scratch = [
        pltpu.VMEM((SUB_TILE + SUBLANES, CONV_DIM), F32),
        pltpu.VMEM((GLA_DV, GLA_HEADS * GLA_DK), F32),
        pltpu.VMEM((SWA_BLOCK, SWA_KV_HEADS * SWA_HEAD_DIM), F32),
        pltpu.VMEM((SWA_BLOCK, SWA_KV_HEADS * SWA_HEAD_DIM), F32),
    ]
    out, w_up_b16, w_down_b16 = pl.pallas_call(
        functools.partial(_mixer_kernel, tok=tok),
        grid=grid,
        in_specs=in_specs,
        out_specs=out_specs,
        out_shape=out_shape,
        scratch_shapes=scratch,
        compiler_params=pltpu.CompilerParams(
            dimension_semantics=("arbitrary", "arbitrary"),
            vmem_limit_bytes=V7X_VMEM_BYTES * 3 // 4),
        name="mixer",
    )(p["sinks"], x, cos, sin, p["n1w"], p["w_in_a"], p["w_in_ga"], p["w_in_b"], p["conv_w"], p["conv_b"],
      p["wa2"], p["ba"], p["onw"], p["qnw"], p["knw"], p["w_branch"], p["w_o"], w_up_f32, w_down_f32)
    return out, w_up_b16, w_down_b16


def _mixer_cast_specs(d, layer, block_of):
    n_blocks = d // CAST_ROWS
    br_rows = N_BRANCHES * BRANCH_WIDTH // n_blocks
    assert n_blocks * CAST_ROWS == d and br_rows % (2 * SUBLANES) == 0
    rows_in = lambda *ids: (layer, block_of(*ids), 0)
    cols_in = lambda *ids: (layer, 0, block_of(*ids))
    rows_out = lambda *ids: (block_of(*ids), 0)
    in_specs = [
        pl.BlockSpec((None, N_IN, CAST_ROWS), cols_in),
        pl.BlockSpec((None, br_rows, d), rows_in),
        pl.BlockSpec((None, CAST_ROWS, d), rows_in),
    ]
    out_specs = [
        pl.BlockSpec((CAST_ROWS, IN_A), rows_out),
        pl.BlockSpec((CAST_ROWS, GA_PAD), rows_out),
        pl.BlockSpec((CAST_ROWS, IN_B), rows_out),
        pl.BlockSpec((br_rows, d), rows_out),
        pl.BlockSpec((CAST_ROWS, d), rows_out),
    ]
    out_shape = [
        jax.ShapeDtypeStruct((d, IN_A), BF16),
        jax.ShapeDtypeStruct((d, GA_PAD), BF16),
        jax.ShapeDtypeStruct((d, IN_B), BF16),
        jax.ShapeDtypeStruct((N_BRANCHES * BRANCH_WIDTH, d), BF16),
        jax.ShapeDtypeStruct((d, d), BF16),
    ]
    return in_specs, out_specs, out_shape


def _mixer_cast_operands(w_in, w_branch, w_o):
    d = w_o.shape[-1]
    return [jnp.swapaxes(w_in, 1, 2), w_branch.reshape(-1, N_BRANCHES * BRANCH_WIDTH, d), w_o]


def _mixer_weight_dict(w_in_a, w_in_ga, w_in_b, w_branch, w_o):
    d = w_o.shape[0]
    return {"w_in_a": w_in_a, "w_in_ga": w_in_ga, "w_in_b": w_in_b,
            "w_branch": w_branch.reshape(N_BRANCHES, BRANCH_WIDTH, d), "w_o": w_o}


def _ffn_call(x2, n2w, w_up, w_down, tile, next_mixer_f32=None, next_layer=None):
    n, d = x2.shape
    n_steps = n // tile
    step_rows = lambda i: (i, 0)
    in_specs = [
        pl.BlockSpec((tile, d), step_rows),
        _const_spec((1, d)),
        _const_spec((d, D_FF)),
        _const_spec((D_FF, d)),
    ]
    out_specs = [pl.BlockSpec((tile, d), step_rows)]
    out_shape = [jax.ShapeDtypeStruct(x2.shape, F32)]
    operands = [x2, n2w, w_up, w_down]
    if next_mixer_f32 is not None:
        w_in_f32, w_branch_f32, w_o_f32 = next_mixer_f32
        n_blocks = d // CAST_ROWS
        assert n_steps % n_blocks == 0
        cast_in, cast_out, cast_shape = _mixer_cast_specs(d, next_layer, lambda i: i // (n_steps // n_blocks))
        in_specs += cast_in
        out_specs += cast_out
        out_shape += cast_shape
        operands += _mixer_cast_operands(w_in_f32, w_branch_f32, w_o_f32)
    outs = pl.pallas_call(
        _ffn_kernel,
        grid=(n_steps,),
        in_specs=in_specs,
        out_specs=out_specs,
        out_shape=out_shape,
        compiler_params=pltpu.CompilerParams(
            dimension_semantics=("arbitrary",),
            vmem_limit_bytes=V7X_VMEM_BYTES * 7 // 8),
        name="ffn",
    )(*operands)
    if next_mixer_f32 is None:
        return outs[0], None
    return outs[0], _mixer_weight_dict(*outs[1:])


def _rope_spread_matrix():
    half = ROPE_DIM // 2
    c = np.arange(LANES)[:, None]
    l = np.arange(LANES)[None, :]
    hit = ((c % half) == (l % half)) & ((l % SWA_HEAD_DIM) < ROPE_DIM)
    sign = np.where((l % SWA_HEAD_DIM) < half, -1.0, 1.0)
    m = np.zeros((2 * LANES, 2 * LANES), np.float32)
    m[:LANES, :LANES] = hit
    m[LANES:, LANES:] = hit * sign
    return jnp.asarray(m, BF16)


def _rope_tables_and_first_weights(positions, w_in, w_branch, w_o):
    bsz, seq = positions.shape
    d = w_o.shape[-1]
    n = bsz * seq
    n_steps = d // CAST_ROWS
    tokens = n // n_steps
    assert tokens * n_steps == n and tokens % (ROPE_TOKENS_PER_ROW * SUBLANES) == 0
    half = ROPE_DIM // 2
    inv_freq = ROPE_THETA ** (-jnp.arange(0, ROPE_DIM, 2, dtype=F32) / ROPE_DIM)
    invf_dense = jnp.tile(inv_freq, LANES // half)[None, :]
    pos_dense = jnp.repeat(positions.astype(jnp.int32).reshape(n), half).reshape(n // ROPE_TOKENS_PER_ROW, LANES)
    rows = tokens // ROPE_TOKENS_PER_ROW
    cast_in, cast_out, cast_shape = _mixer_cast_specs(d, 0, lambda i: i)
    outs = pl.pallas_call(
        functools.partial(_rope_table_kernel, tokens=tokens),
        grid=(n_steps,),
        in_specs=[
            pl.BlockSpec((rows, LANES), lambda i: (i, 0)),
            _const_spec((1, LANES)),
            _const_spec((2 * LANES, 2 * LANES)),
        ] + cast_in,
        out_specs=[pl.BlockSpec((tokens, LANES), lambda i: (i, 0))] * 2 + cast_out,
        out_shape=[jax.ShapeDtypeStruct((n, LANES), F32)] * 2 + cast_shape,
        compiler_params=pltpu.CompilerParams(
            dimension_semantics=("arbitrary",),
            vmem_limit_bytes=V7X_VMEM_BYTES * 3 // 4),
        name="rope_tables",
    )(pos_dense, invf_dense, _rope_spread_matrix(), *_mixer_cast_operands(w_in, w_branch, w_o))
    cos, sin = outs[0], outs[1]
    return cos.reshape(bsz, seq, LANES), sin.reshape(bsz, seq, LANES), _mixer_weight_dict(*outs[2:])


def _small_mixer_params(layer, norm1_w, conv_w, conv_b, gla_wa2, gla_ba, gla_onorm_w, q_norm_w, k_norm_w, sinks):
    heads_per_tile = LANES // SWA_HEAD_DIM
    return {
        "sinks": sinks[layer].astype(F32),
        "n1w": norm1_w[layer][None, :],
        "conv_w": conv_w[layer],
        "conv_b": conv_b[layer][None, :],
        "wa2": jnp.pad(gla_wa2[layer], ((0, GA_PAD - GLA_RANK), (0, 0))).astype(BF16),
        "ba": gla_ba[layer][None, :],
        "onw": gla_onorm_w[layer][None, :],
        "qnw": jnp.tile(q_norm_w[layer], heads_per_tile)[None, :],
        "knw": jnp.tile(k_norm_w[layer], heads_per_tile)[None, :],
    }


def kernel(x, positions, norm1_w, w_in, conv_w, conv_b, gla_wa2, gla_ba, gla_onorm_w, q_norm_w, k_norm_w,
           sinks, w_branch, w_o, norm2_w, w_up, w_down):
    bsz, seq, d = x.shape
    depth = w_in.shape[0]
    assert d == D_MODEL and w_in.shape[2] == N_IN
    tok = min(TOK_TILE, seq)
    assert seq % tok == 0 and tok % SUB_TILE == 0 and SUB_TILE % (2 * SWA_BLOCK) == 0
    ffn_tile = min(FFN_TILE, bsz * seq)
    assert (bsz * seq) % ffn_tile == 0
    cos, sin, mixer_w = _rope_tables_and_first_weights(positions, w_in, w_branch, w_o)
    for layer in range(depth):
        p = dict(mixer_w, **_small_mixer_params(layer, norm1_w, conv_w, conv_b, gla_wa2, gla_ba, gla_onorm_w,
                                                q_norm_w, k_norm_w, sinks))
        x, w_up_b16, w_down_b16 = _mixer_call(x, cos, sin, p, w_up, w_down, layer, tok)
        nxt = (w_in, w_branch, w_o) if layer + 1 < depth else None
        x2, mixer_w = _ffn_call(x.reshape(bsz * seq, d), norm2_w[layer][None, :], w_up_b16, w_down_b16, ffn_tile,
                                nxt, layer + 1)
        x = x2.reshape(bsz, seq, d)
    return x
```
